```python
import jax, jax.numpy as jnp
from jax import lax
import numpy as np

D_MODEL = 2048
BATCH = 8
SEQ = 4096
DEPTH = 4

GRID_W = 64
CTX_LEN = 256
EPS = 1e-6
D_FF = 11 * D_MODEL // 4
N_MOD = 9
N_BRANCH = 3
A_HEADS = 4
A_W = D_MODEL // 2
A_DH = A_W // A_HEADS
B_DK = 128
B_W = D_MODEL // 2
B_HEADS = B_W // B_DK
C_W = D_MODEL // 2
C_BLOCKS = 16
C_DB = C_W // C_BLOCKS
CONV_W = 4
CONV_PAD = (2, 1)
RG_C = 8.0
CHUNK = 64
IN_SPLITS = (A_W, A_W, A_W, A_W, 4 * A_HEADS,
             B_W, B_W, B_W, B_W, B_W,
             C_W, C_W,
             N_BRANCH * D_MODEL)
IN_OFFSETS = tuple(int(v) for v in np.cumsum(IN_SPLITS)[:-1])
IN_WIDTH = int(sum(IN_SPLITS))

kernel_name = 'hybrid_mlstm_hgrn2_rglru_flow_block'


def rmsnorm(x, g):
    xf = x.astype(jnp.float32)
    y = xf * lax.rsqrt(jnp.mean(xf * xf, axis=-1, keepdims=True) + EPS)
    return y.astype(x.dtype) * g


def head_rms(x):
    return x * lax.rsqrt(jnp.mean(x * x, axis=-1, keepdims=True) + EPS)


def adaln(x, g, shift, scale):
    return rmsnorm(x, g) * (1 + scale) + shift


def swiglu(x, w_i, w_o):
    gate, up = jnp.split(x @ w_i, 2, axis=-1)
    return (jax.nn.silu(gate) * up) @ w_o


def to_heads(t, n):
    b, s, _ = t.shape
    return t.reshape(b, s, n, -1).transpose(0, 2, 1, 3).astype(jnp.float32)


def from_heads(t):
    b, n, s, d = t.shape
    return t.transpose(0, 2, 1, 3).reshape(b, s, n * d)


def to_chunks(t):
    b, n, s = t.shape[:3]
    t = t.reshape(b, n, s // CHUNK, CHUNK, *t.shape[3:])
    return jnp.moveaxis(t, 2, 0)


def from_chunks(t):
    t = jnp.moveaxis(t, 0, 2)
    b, n, nc, l = t.shape[:4]
    return t.reshape(b, n, nc * l, *t.shape[4:])


def mlstm_scan(q, k, v, logf, logi, state):
    mask = jnp.tril(jnp.ones((CHUNK, CHUNK), dtype=bool))

    def step(carry, xs):
        cmat, nvec, m = carry
        qc, kc, vc, fc, ic = xs
        b = jnp.cumsum(fc, axis=-1)
        dmat = jnp.where(mask, b[..., :, None] - b[..., None, :] + ic[..., None, :], -jnp.inf)
        inter = b + m[..., None]
        m_t = jnp.maximum(inter, jnp.max(dmat, axis=-1))
        w_inter = jnp.exp(inter - m_t)
        s = jnp.einsum('bhtd,bhsd->bhts', qc, kc) * jnp.exp(dmat - m_t[..., None])
        num = (w_inter[..., None] * jnp.einsum('bhtd,bhde->bhte', qc, cmat)
               + jnp.einsum('bhts,bhse->bhte', s, vc))
        den = w_inter * jnp.einsum('bhtd,bhd->bht', qc, nvec) + jnp.sum(s, axis=-1)
        h_out = num / jnp.maximum(jnp.abs(den), jnp.exp(-m_t))[..., None]
        b_last = b[..., -1]
        wlog = b_last[..., None] - b + ic
        m_new = jnp.maximum(b_last + m, jnp.max(wlog, axis=-1))
        decay = jnp.exp(b_last + m - m_new)
        ws = jnp.exp(wlog - m_new[..., None])[..., None]
        cmat = decay[..., None, None] * cmat + jnp.einsum('bhsd,bhse->bhde', kc * ws, vc)
        nvec = decay[..., None] * nvec + jnp.sum(kc * ws, axis=2)
        return (cmat, nvec, m_new), h_out

    state, hs = lax.scan(step, state, tuple(to_chunks(t) for t in (q, k, v, logf, logi)))
    return from_chunks(hs), state


def hgrn_scan(q, k, v, g, state):
    mask = jnp.tril(jnp.ones((CHUNK, CHUNK), dtype=bool))[:, :, None]

    def step(s_mat, xs):
        qc, kc, vc, gc = xs
        b = jnp.cumsum(gc, axis=2)
        diff = jnp.where(mask, b[:, :, :, None, :] - b[:, :, None, :, :], -jnp.inf)
        att = jnp.sum(qc[:, :, :, None, :] * kc[:, :, None, :, :] * jnp.exp(diff), axis=-1)
        o = (jnp.einsum('bhtd,bhde->bhte', qc * jnp.exp(b), s_mat)
             + jnp.einsum('bhts,bhse->bhte', att, vc))
        b_last = b[:, :, -1:, :]
        s_mat = (jnp.exp(b_last[:, :, 0, :, None]) * s_mat
                 + jnp.einsum('bhsd,bhse->bhde', kc * jnp.exp(b_last - b), vc))
        return s_mat, o

    state, os_ = lax.scan(step, state, tuple(to_chunks(t) for t in (q, k, v, g)))
    return from_chunks(os_), state


def rglru_scan(a, bx, h0):
    bx = bx.at[:, 0].add(a[:, 0] * h0)

    def combine(e1, e2):
        a1, b1 = e1
        a2, b2 = e2
        return a1 * a2, a2 * b1 + b2

    _, hs = lax.associative_scan(combine, (a, bx), axis=1)
    return hs, hs[:, -1]


def run_bidir(scan_fn, axis, init, ctx_f, lat_f, ctx_b, lat_b):
    flip = lambda t: jnp.flip(t, axis)
    oc_f, s_f = scan_fn(*ctx_f, init)
    ol_f, _ = scan_fn(*lat_f, s_f)
    oc_b, s_b = scan_fn(*[flip(t) for t in ctx_b], init)
    ol_b, _ = scan_fn(*[flip(t) for t in lat_b], s_b)
    return oc_f + flip(oc_b), ol_f + flip(ol_b)


def token_mixers(xn_l, xn_c, rows, w_in_l, gate_b_l, lb_l, conv_w_l, conv_b_l, rgw_l, rgb_l,
                 lam_l, branch_w_l, w_out_l, need_ctx):
    f32 = jnp.float32
    nb = xn_l.shape[0]
    pl = jnp.split(xn_l @ w_in_l, IN_OFFSETS, axis=-1)
    pc = jnp.split(xn_c @ w_in_l, IN_OFFSETS, axis=-1)

    def mlstm_inputs(p):
        bsz, s = p[0].shape[:2]
        q = to_heads(p[0], A_HEADS)
        k = to_heads(p[1], A_HEADS) * (A_DH ** -0.5)
        v = to_heads(p[2], A_HEADS)
        g = p[4].astype(f32).reshape(bsz, s, 2, 2, A_HEADS) + gate_b_l
        g = g.transpose(2, 3, 0, 4, 1)
        return ((q, k, v, jax.nn.log_sigmoid(g[0, 1]), g[0, 0]),
                (q, k, v, jax.nn.log_sigmoid(g[1, 1]), g[1, 0]))

    (acf, acb), (alf, alb) = mlstm_inputs(pc), mlstm_inputs(pl)
    a_init = (jnp.zeros((nb, A_HEADS, A_DH, A_DH), f32), jnp.zeros((nb, A_HEADS, A_DH), f32),
              jnp.zeros((nb, A_HEADS), f32))
    hc_a, hl_a = run_bidir(mlstm_scan, 2, a_init, acf, alf, acb, alb)

    def mlstm_out(hsum, p):
        return (from_heads(head_rms(hsum)) * jax.nn.sigmoid(p[3].astype(f32))).astype(p[3].dtype)

    def hgrn_inputs(p):
        q = jax.nn.silu(to_heads(p[5], B_HEADS))
        v = to_heads(p[8], B_HEADS)
        dirs = []
        for d, fpre in enumerate((p[6], p[7])):
            f = lb_l[d] + (1 - lb_l[d]) * jax.nn.sigmoid(fpre.astype(f32))
            f = to_heads(f, B_HEADS)
            dirs.append((q, 1 - f, v, jnp.log(f)))
        return dirs

    (bcf, bcb), (blf, blb) = hgrn_inputs(pc), hgrn_inputs(pl)
    b_init = jnp.zeros((nb, B_HEADS, B_DK, B_DK), f32)
    hc_b, hl_b = run_bidir(hgrn_scan, 2, b_init, bcf, blf, bcb, blb)

    def hgrn_out(osum, p):
        return (from_heads(head_rms(osum)) * jax.nn.silu(p[9].astype(f32))).astype(p[9].dtype)

    def to_cols(t):
        b, s, ch = t.shape
        return t.reshape(b, rows, GRID_W, ch).transpose(0, 2, 1, 3).reshape(b, s, ch)

    def from_cols(t):
        b, s, ch = t.shape
        return t.reshape(b, GRID_W, rows, ch).transpose(0, 2, 1, 3).reshape(b, s, ch)

    def conv(t):
        y = lax.conv_general_dilated(t, conv_w_l[:, None, :], window_strides=(1,), padding=(CONV_PAD,),
                                     dimension_numbers=('NWC', 'WIO', 'NWC'), feature_group_count=C_W)
        return y + conv_b_l

    def rg_inputs(t):
        bsz, s, _ = t.shape
        tf = t.astype(f32)
        blk = tf.reshape(bsz, s, C_BLOCKS, C_DB)
        dirs = []
        for d in range(2):
            r = jax.nn.sigmoid(jnp.einsum('btni,nij->btnj', blk, rgw_l[d, 0]).reshape(bsz, s, C_W) + rgb_l[d, 0])
            i = jax.nn.sigmoid(jnp.einsum('btni,nij->btnj', blk, rgw_l[d, 1]).reshape(bsz, s, C_W) + rgb_l[d, 1])
            log_a = -RG_C * r * jax.nn.softplus(-lam_l[d].astype(f32))
            dirs.append((jnp.exp(log_a), jnp.sqrt(-jnp.expm1(2 * log_a)) * (i * tf)))
        return dirs

    (ccf, ccb), (clf, clb) = rg_inputs(conv(pc[10])), rg_inputs(conv(to_cols(pl[10])))
    hc_c, hl_c = run_bidir(rglru_scan, 1, jnp.zeros((nb, C_W), f32), ccf, clf, ccb, clb)

    def merge(ya, yb, yc, p):
        ga, gb, gc = jnp.split(jax.nn.sigmoid(p[12]), N_BRANCH, axis=-1)
        merged = ga * (ya @ branch_w_l[0]) + gb * (yb @ branch_w_l[1]) + gc * (yc @ branch_w_l[2])
        return merged @ w_out_l

    yc_l = (from_cols(hl_c) * jax.nn.gelu(pl[11].astype(f32))).astype(pl[11].dtype)
    y_lat = merge(mlstm_out(hl_a, pl), hgrn_out(hl_b, pl), yc_l, pl)
    y_ctx = None
    if need_ctx:
        yc_c = (hc_c * jax.nn.gelu(pc[11].astype(f32))).astype(pc[11].dtype)
        y_ctx = merge(mlstm_out(hc_a, pc), hgrn_out(hc_b, pc), yc_c, pc)
    return y_lat, y_ctx


def setup_inputs(seed: int = 0) -> dict:
    key = jax.random.key(seed)
    ks = jax.random.split(key, 24)
    f32 = jnp.float32
    nrm = lambda k, shape, s: jax.random.normal(k, shape, f32) * s
    x = nrm(ks[0], (BATCH, SEQ, D_MODEL), 1.0)
    c = nrm(ks[1], (BATCH, D_MODEL), 1.0)
    ctx = nrm(ks[2], (BATCH, CTX_LEN, D_MODEL), 1.0)
    c_ctx = nrm(ks[3], (D_MODEL,), 1.0)
    mod_w = nrm(ks[4], (DEPTH, D_MODEL, N_MOD * D_MODEL), 0.5 * D_MODEL ** -0.5)
    mod_b = nrm(ks[5], (DEPTH, N_MOD * D_MODEL), 0.02)
    norm_g = 1.0 + nrm(ks[6], (DEPTH, 3, D_MODEL), 0.02)
    ffn_w_in = nrm(ks[7], (DEPTH, 2, D_MODEL, 2 * D_FF), D_MODEL ** -0.5)
    ffn_w_out = nrm(ks[8], (DEPTH, 2, D_FF, D_MODEL), D_FF ** -0.5)
    w_in = nrm(ks[9], (DEPTH, D_MODEL, IN_WIDTH), D_MODEL ** -0.5)
    i_bias = nrm(ks[10], (DEPTH, 2, 1, A_HEADS), 0.1)
    f_bias = jnp.linspace(3.0, 6.0, A_HEADS, dtype=f32) + nrm(ks[11], (DEPTH, 2, 1, A_HEADS), 0.1)
    mlstm_gate_b = jnp.concatenate([i_bias, f_bias], axis=2)
    hgrn_lb_logits = nrm(ks[12], (DEPTH, 2, B_W), 0.1)
    conv_w = nrm(ks[13], (DEPTH, CONV_W, C_W), CONV_W ** -0.5)
    conv_b = nrm(ks[14], (DEPTH, C_W), 0.02)
    rg_gate_w = nrm(ks[15], (DEPTH, 2, 2, C_BLOCKS, C_DB, C_DB), C_DB ** -0.5)
    rg_gate_b = nrm(ks[16], (DEPTH, 2, 2, C_W), 0.02)
    u = jax.random.uniform(ks[17], (DEPTH, 2, C_W), f32, 0.9, 0.999)
    base = u ** (1.0 / RG_C)
    rg_lambda = jnp.log(base) - jnp.log1p(-base)
    branch_w = nrm(ks[18], (DEPTH, N_BRANCH, A_W, D_MODEL), A_W ** -0.5)
    w_out = nrm(ks[19], (DEPTH, D_MODEL, D_MODEL), D_MODEL ** -0.5)
    final_g = 1.0 + nrm(ks[20], (D_MODEL,), 0.02)
    return {'x': x, 'c': c, 'ctx': ctx, 'c_ctx': c_ctx, 'mod_w': mod_w, 'mod_b': mod_b,
            'norm_g': norm_g, 'ffn_w_in': ffn_w_in, 'ffn_w_out': ffn_w_out, 'w_in': w_in,
            'mlstm_gate_b': mlstm_gate_b, 'hgrn_lb_logits': hgrn_lb_logits, 'conv_w': conv_w,
            'conv_b': conv_b, 'rg_gate_w': rg_gate_w, 'rg_gate_b': rg_gate_b, 'rg_lambda': rg_lambda,
            'branch_w': branch_w, 'w_out': w_out, 'final_g': final_g}


def reference(x, c, ctx, c_ctx, mod_w, mod_b, norm_g, ffn_w_in, ffn_w_out, w_in, mlstm_gate_b,
              hgrn_lb_logits, conv_w, conv_b, rg_gate_w, rg_gate_b, rg_lambda, branch_w, w_out, final_g):
    rows = x.shape[1] // GRID_W
    lb_p = jax.nn.softmax(hgrn_lb_logits.astype(jnp.float32), axis=0)
    lb_all = jnp.cumsum(lb_p, axis=0) - lb_p[0:1]
    sc = jax.nn.silu(c)[:, None, :]
    scc = jax.nn.silu(c_ctx)[None, None, :]
    h, hc = x, ctx
    for layer in range(DEPTH):
        last = layer == DEPTH - 1
        ml = jnp.split(sc @ mod_w[layer] + mod_b[layer], N_MOD, axis=-1)
        mc = jnp.split(scc @ mod_w[layer] + mod_b[layer], N_MOD, axis=-1)
        h = h + 0.5 * ml[2] * swiglu(adaln(h, norm_g[layer, 0], ml[0], ml[1]), ffn_w_in[layer, 0], ffn_w_out[layer, 0])
        hc = hc + 0.5 * mc[2] * swiglu(adaln(hc, norm_g[layer, 0], mc[0], mc[1]), ffn_w_in[layer, 0], ffn_w_out[layer, 0])
        y_lat, y_ctx = token_mixers(adaln(h, norm_g[layer, 1], ml[3], ml[4]), adaln(hc, norm_g[layer, 1], mc[3], mc[4]),
                                    rows, w_in[layer], mlstm_gate_b[layer], lb_all[layer], conv_w[layer],
                                    conv_b[layer], rg_gate_w[layer], rg_gate_b[layer], rg_lambda[layer],
                                    branch_w[layer], w_out[layer], not last)
        h = h + ml[5] * y_lat
        h = h + 0.5 * ml[8] * swiglu(adaln(h, norm_g[layer, 2], ml[6], ml[7]), ffn_w_in[layer, 1], ffn_w_out[layer, 1])
        if not last:
            hc = hc + mc[5] * y_ctx
            hc = hc + 0.5 * mc[8] * swiglu(adaln(hc, norm_g[layer, 2], mc[6], mc[7]), ffn_w_in[layer, 1], ffn_w_out[layer, 1])
    return rmsnorm(h, final_g)
```

```python
import functools

import numpy as np
import jax
import jax.numpy as jnp
from jax import lax
from jax.experimental import pallas as pl
from jax.experimental.pallas import tpu as pltpu

EPS = 1e-6
N_MOD = 9
GRID_W = 64
A_HEADS = 4
B_DK = 128
C_BLOCKS = 16
CONV_W = 4
RG_C = 8.0
NEG_BIG = -1e30

MOD_ROWS = 16
GATE_PAD = 128
ROW_TILE = 512
FF_TILE = 512
PROJ_TILE = 1024
MOD_TILE = 1024
MERGE_TILE = 512
MERGE_ROWS = 256
LA = 256
LB = 128
RG_SLAB = 256
RG_ROWS = 512
VMEM_LIMIT = 56 * 1024 * 1024

F32 = jnp.float32
BF16 = jnp.bfloat16


def _cparams(sem):
    return pltpu.CompilerParams(dimension_semantics=sem, vmem_limit_bytes=VMEM_LIMIT)


def _dot(a, b):
    return jnp.dot(a, b, preferred_element_type=F32)


def _dot_nt(a, b):
    return lax.dot_general(a, b, (((1,), (1,)), ((), ())), preferred_element_type=F32)


def _split3(x):
    hi = x.astype(BF16)
    r1 = x - hi.astype(F32)
    mid = r1.astype(BF16)
    lo = (r1 - mid.astype(F32)).astype(BF16)
    return hi, mid, lo


def _dot01(m, x):
    hi, mid, lo = _split3(x)
    return _dot(m, hi) + _dot(m, mid) + _dot(m, lo)


def _sigmoid(x):
    return jax.nn.sigmoid(x)


def _softplus(z):
    return jnp.maximum(z, 0.0) + jnp.log1p(jnp.exp(-jnp.abs(z)))


def _gelu_tanh(x):
    return 0.5 * x * (1.0 + jnp.tanh(np.sqrt(2.0 / np.pi).astype(np.float32) * (x + 0.044715 * (x * x * x))))


def _adaln(h, g, shift, scale):
    y = h * lax.rsqrt(jnp.mean(h * h, axis=-1, keepdims=True) + EPS)
    return (y * g) * (1.0 + scale) + shift


def _mod_kernel(c_ref, w_ref, b_ref, o_ref):
    cv = c_ref[...]
    sc = (cv * _sigmoid(cv)).astype(BF16)
    o_ref[...] = _dot(sc, w_ref[...].astype(BF16)) + b_ref[...]


def _mod_table(c16, mod_w, mod_b):
    depth, d, n = mod_w.shape
    return pl.pallas_call(
        _mod_kernel,
        grid=(depth, n // MOD_TILE),
        in_specs=[
            pl.BlockSpec((MOD_ROWS, d), lambda l, j: (0, 0)),
            pl.BlockSpec((None, d, MOD_TILE), lambda l, j: (l, 0, j)),
            pl.BlockSpec((None, 1, MOD_TILE), lambda l, j: (l, 0, j)),
        ],
        out_specs=pl.BlockSpec((None, MOD_ROWS, MOD_TILE), lambda l, j: (l, 0, j)),
        out_shape=jax.ShapeDtypeStruct((depth, MOD_ROWS, n), F32),
        compiler_params=_cparams(("arbitrary", "arbitrary")),
        name="mod_table",
    )(c16, mod_w, mod_b.reshape(depth, 1, n))


def _ffn_kernel(h_ref, mod_ref, g_ref, wg_ref, wu_ref, wo_ref, *rest, mod_base, final):
    if final:
        fg_ref, o_ref, xn_ref, acc_ref = rest
    else:
        o_ref, xn_ref, acc_ref = rest
    j = pl.program_id(1)

    @pl.when(j == 0)
    def _():
        xn = _adaln(h_ref[...], g_ref[...], mod_ref[mod_base:mod_base + 1, :],
                    mod_ref[mod_base + 1:mod_base + 2, :])
        xn_ref[...] = xn.astype(BF16)
        acc_ref[...] = jnp.zeros_like(acc_ref)

    xn = xn_ref[...]
    gate = _dot(xn, wg_ref[...])
    up = _dot(xn, wu_ref[...])
    act = (gate * _sigmoid(gate)) * up
    acc_ref[...] += _dot(act.astype(BF16), wo_ref[...])

    @pl.when(j == pl.num_programs(1) - 1)
    def _():
        out = h_ref[...] + (0.5 * mod_ref[mod_base + 2:mod_base + 3, :]) * acc_ref[...]
        if final:
            out = out * lax.rsqrt(jnp.mean(out * out, axis=-1, keepdims=True) + EPS) * fg_ref[...]
        o_ref[...] = out


def _ffn(h, n_rows, modtab, mod_row, g, w_in, w_out, layer, which, mod_base, final_g=None):
    d = h.shape[1]
    d_ff = w_out.shape[2]
    n_ff = d_ff // FF_TILE
    final = final_g is not None
    in_specs = [
        pl.BlockSpec((ROW_TILE, d), lambda i, j: (i, 0)),
        pl.BlockSpec((None, N_MOD, d), lambda i, j: (mod_row(ROW_TILE)(i), 0, 0)),
        pl.BlockSpec((1, d), lambda i, j: (0, 0)),
        pl.BlockSpec((None, None, d, FF_TILE), lambda i, j: (layer, which, 0, j)),
        pl.BlockSpec((None, None, d, FF_TILE), lambda i, j: (layer, which, 0, j + n_ff)),
        pl.BlockSpec((None, None, FF_TILE, d), lambda i, j: (layer, which, j, 0)),
    ]
    args = [h, modtab, g, w_in, w_in, w_out]
    if final:
        in_specs.append(pl.BlockSpec((1, d), lambda i, j: (0, 0)))
        args.append(final_g)
    return pl.pallas_call(
        functools.partial(_ffn_kernel, mod_base=mod_base, final=final),
        grid=(n_rows // ROW_TILE, n_ff),
        in_specs=in_specs,
        out_specs=pl.BlockSpec((ROW_TILE, d), lambda i, j: (i, 0)),
        out_shape=jax.ShapeDtypeStruct((n_rows, d), F32),
        scratch_shapes=[pltpu.VMEM((ROW_TILE, d), BF16), pltpu.VMEM((ROW_TILE, d), F32)],
        compiler_params=_cparams(("parallel", "arbitrary")),
        name="ffn",
    )(*args)


def _inproj_kernel(h_ref, mod_ref, g_ref, w_ref, wgate_ref, p_ref, gate_ref, xn_ref, *, mod_base):
    @pl.when(pl.program_id(1) == 0)
    def _():
        xn = _adaln(h_ref[...], g_ref[...], mod_ref[mod_base:mod_base + 1, :],
                    mod_ref[mod_base + 1:mod_base + 2, :]).astype(BF16)
        xn_ref[...] = xn
        gate_ref[...] = _dot(xn, wgate_ref[...])

    p_ref[...] = _dot(xn_ref[...], w_ref[...])


def _inproj(h, modtab, mod_row, g, w_main, w_gate, layer, mod_base):
    t, d = h.shape
    n = w_main.shape[2]
    return pl.pallas_call(
        functools.partial(_inproj_kernel, mod_base=mod_base),
        grid=(t // ROW_TILE, n // PROJ_TILE),
        in_specs=[
            pl.BlockSpec((ROW_TILE, d), lambda i, j: (i, 0)),
            pl.BlockSpec((None, N_MOD, d), lambda i, j: (mod_row(ROW_TILE)(i), 0, 0)),
            pl.BlockSpec((1, d), lambda i, j: (0, 0)),
            pl.BlockSpec((None, d, PROJ_TILE), lambda i, j: (layer, 0, j)),
            pl.BlockSpec((None, d, GATE_PAD), lambda i, j: (layer, 0, 0)),
        ],
        out_specs=[
            pl.BlockSpec((ROW_TILE, PROJ_TILE), lambda i, j: (i, j)),
            pl.BlockSpec((ROW_TILE, GATE_PAD), lambda i, j: (i, 0)),
        ],
        out_shape=[jax.ShapeDtypeStruct((t, n), F32), jax.ShapeDtypeStruct((t, GATE_PAD), F32)],
        scratch_shapes=[pltpu.VMEM((ROW_TILE, d), BF16)],
        compiler_params=_cparams(("parallel", "arbitrary")),
        name="inproj",
    )(h, modtab, g, w_main, w_gate)


def _mlstm_kernel(qf_ref, kf_ref, vf_ref, gf_ref, qb_ref, kb_ref, vb_ref, gb_ref, bias_ref,
                  trif_ref, trib_ref, of_ref, ob_ref, c_ref, n_ref, m_ref, *, dh):
    length = qf_ref.shape[0]

    @pl.when(pl.program_id(1) == 0)
    def _():
        c_ref[...] = jnp.zeros_like(c_ref)
        n_ref[...] = jnp.zeros_like(n_ref)
        m_ref[...] = jnp.zeros_like(m_ref)

    row = lax.broadcasted_iota(jnp.int32, (length, length), 0)
    col = lax.broadcasted_iota(jnp.int32, (length, length), 1)
    dirs = ((qf_ref, kf_ref, vf_ref, gf_ref, trif_ref, of_ref, col <= row, length - 1),
            (qb_ref, kb_ref, vb_ref, gb_ref, trib_ref, ob_ref, col >= row, 0))
    for d, (q_ref, k_ref, v_ref, g_ref, tri_ref, o_ref, mask, last) in enumerate(dirs):
        pre = g_ref[...] + bias_ref[...]
        logf = jnp.minimum(pre, 0.0) - jnp.log1p(jnp.exp(-jnp.abs(pre)))
        b_all = _dot01(tri_ref[...], logf)
        b_all_t = b_all.T
        pre_t = pre.T
        for hh in range(A_HEADS):
            ci = d * 2 * A_HEADS + hh
            cf = ci + A_HEADS
            sl = slice(hh * dh, (hh + 1) * dh)
            idx = d * A_HEADS + hh
            b_col = b_all[:, cf:cf + 1]
            b_row = b_all_t[cf:cf + 1, :]
            i_col = pre[:, ci:ci + 1]
            i_row = pre_t[ci:ci + 1, :]
            m_prev = m_ref[idx][:, 0:1]
            cmat = c_ref[idx]
            nvec = n_ref[idx]

            dmat = jnp.where(mask, b_col - b_row + i_row, NEG_BIG)
            inter = b_col + m_prev
            m_t = jnp.maximum(inter, jnp.max(dmat, axis=-1, keepdims=True))
            w_inter = jnp.exp(inter - m_t)
            qf = q_ref[:, sl]
            kf = k_ref[:, sl] * (dh ** -0.5)
            q = qf.astype(BF16)
            v = v_ref[:, sl].astype(BF16)
            s = _dot_nt(q, kf.astype(BF16)) * jnp.exp(dmat - m_t)
            num = w_inter * _dot(q, cmat.astype(BF16)) + _dot(s.astype(BF16), v)
            den = (w_inter * jnp.sum(qf * nvec, axis=-1, keepdims=True)
                   + jnp.sum(s, axis=-1, keepdims=True))
            o_ref[:, sl] = num / jnp.maximum(jnp.abs(den), jnp.exp(-m_t))

            b_last = b_all[last:last + 1, cf:cf + 1]
            wlog = b_last - b_col + i_col
            m_new = jnp.maximum(b_last + m_prev, jnp.max(wlog, axis=0, keepdims=True))
            decay = jnp.exp(b_last + m_prev - m_new)
            kw = kf * jnp.exp(wlog - m_new)
            c_ref[idx] = decay * cmat + _dot(kw.T.astype(BF16), v)
            n_ref[idx] = decay * nvec + jnp.sum(kw, axis=0, keepdims=True)
            m_ref[idx] = jnp.broadcast_to(m_new, m_ref.shape[1:])


def _mlstm(p, gates, gate_bias, tri_f, tri_b, n_batch, seq, ctx_len, dh):
    t = p.shape[0]
    aw = A_HEADS * dh
    n_lat = seq // LA
    n_ctx = ctx_len // LA
    steps = n_ctx + n_lat
    ctx_base = n_batch * seq // LA

    def fwd_block(b, s):
        return jnp.where(s < n_ctx, ctx_base + b * n_ctx + s, b * n_lat + (s - n_ctx))

    def bwd_block(b, s):
        return jnp.where(s < n_ctx, ctx_base + b * n_ctx + (n_ctx - 1 - s),
                         b * n_lat + (n_lat - 1 - (s - n_ctx)))

    def spec(block_fn, colblk, width):
        return pl.BlockSpec((LA, width), lambda b, s: (block_fn(b, s), colblk))

    const = lambda shape: pl.BlockSpec(shape, lambda b, s: (0,) * len(shape))
    return pl.pallas_call(
        functools.partial(_mlstm_kernel, dh=dh),
        grid=(n_batch, steps),
        in_specs=[spec(fwd_block, 0, aw), spec(fwd_block, 1, aw), spec(fwd_block, 2, aw),
                  spec(fwd_block, 0, GATE_PAD),
                  spec(bwd_block, 0, aw), spec(bwd_block, 1, aw), spec(bwd_block, 2, aw),
                  spec(bwd_block, 0, GATE_PAD),
                  const((1, GATE_PAD)), const((LA, LA)), const((LA, LA))],
        out_specs=[spec(fwd_block, 0, aw), spec(bwd_block, 0, aw)],
        out_shape=[jax.ShapeDtypeStruct((t, aw), F32)] * 2,
        scratch_shapes=[pltpu.VMEM((2 * A_HEADS, dh, dh), F32),
                        pltpu.VMEM((2 * A_HEADS, 1, dh), F32),
                        pltpu.VMEM((2 * A_HEADS, 1, 128), F32)],
        compiler_params=_cparams(("parallel", "arbitrary")),
        name="mlstm",
    )(p, p, p, gates, p, p, p, gates, gate_bias, tri_f, tri_b)


def _hgrn_levels(length):
    levels = []
    c = length // 2
    while c >= 1:
        levels.append(c)
        c //= 2
    return levels


def _hgrn_constants(length):
    t = np.arange(length)[:, None]
    u = np.arange(length)[None, :]
    out = []
    for direction in (0, 1):
        mats, masks = [], []
        if direction == 0:
            mats.append(u <= t)
            mats.append(u > t)
        else:
            mats.append(u >= t)
            mats.append(u < t)
        for c in _hgrn_levels(length):
            blk = t // (2 * c)
            pos = t % (2 * c)
            if direction == 0:
                ref = blk * 2 * c + c - 1
                q_role = pos >= c
                m = np.where(q_role, (u > ref) & (u <= t), (u > t) & (u <= ref))
                keep = (blk == blk.T) & q_role & (pos.T < c)
            else:
                ref = blk * 2 * c + c
                q_role = pos < c
                m = np.where(q_role, (u >= t) & (u < ref), (u >= ref) & (u < t))
                keep = (blk == blk.T) & q_role & (pos.T >= c)
            mats.append(m)
            masks.append(keep)
        out.append((np.concatenate(mats, 0).astype(np.float32), np.stack(masks).astype(np.float32)))
    return out


def _hgrn_kernel(qf_ref, ff_ref, vf_ref, qb_ref, fb_ref, vb_ref, lb_ref, mf_ref, mb_ref,
                 maskf_ref, maskb_ref, of_ref, ob_ref, st_ref, d_ref):
    length, width = qf_ref.shape
    n_heads = width // B_DK
    levels = _hgrn_levels(length)

    @pl.when(pl.program_id(1) == 0)
    def _():
        st_ref[...] = jnp.zeros_like(st_ref)

    t_idx = lax.broadcasted_iota(jnp.int32, (length, 1), 0)
    eye = (lax.broadcasted_iota(jnp.int32, (length, length), 0)
           == lax.broadcasted_iota(jnp.int32, (length, length), 1))
    dirs = ((qf_ref, ff_ref, vf_ref, mf_ref, maskf_ref, of_ref, length - 1),
            (qb_ref, fb_ref, vb_ref, mb_ref, maskb_ref, ob_ref, 0))
    for d, (q_ref, f_ref, v_ref, m_ref, mask_ref, o_ref, last) in enumerate(dirs):
        lb = lb_ref[d:d + 1, :]
        f = lb + (1.0 - lb) * _sigmoid(f_ref[...])
        kk = 1.0 - f
        qpre = q_ref[...]
        q = qpre * _sigmoid(qpre)
        d_ref[...] = _dot01(m_ref[...], jnp.log(f))
        run = d_ref[0:length, :]
        q_in = (q * jnp.exp(run)).astype(BF16)
        k_st = kk * jnp.exp(d_ref[length:2 * length, :])
        decay = jnp.exp(d_ref[last:last + 1, :])
        xs = []
        for li, c in enumerate(levels):
            pos = t_idx % (2 * c)
            q_role = (pos >= c) if d == 0 else (pos < c)
            e = jnp.exp(d_ref[(2 + li) * length:(3 + li) * length, :])
            xs.append((jnp.where(q_role, q, kk) * e).astype(BF16))
        qk = q * kk
        v_all = v_ref[...]
        for hd in range(n_heads):
            sl = slice(hd * B_DK, (hd + 1) * B_DK)
            idx = d * n_heads + hd
            att = jnp.where(eye, jnp.sum(qk[:, sl], axis=-1, keepdims=True), 0.0)
            for li in range(len(levels)):
                x = xs[li][:, sl]
                att = att + mask_ref[li] * _dot_nt(x, x)
            vh = v_all[:, sl]
            st = st_ref[idx]
            o_ref[:, sl] = _dot(att.astype(BF16), vh.astype(BF16)) + _dot_nt(q_in[:, sl], st.astype(BF16))
            st_ref[idx] = st * decay[:, sl] + _dot(vh.T.astype(BF16), k_st[:, sl].astype(BF16))


def _hgrn(p, lb, consts, n_batch, seq, ctx_len, col0):
    t = p.shape[0]
    bw = lb.shape[1]
    n_lat = seq // LB
    n_ctx = ctx_len // LB
    steps = n_ctx + n_lat
    ctx_base = n_batch * seq // LB
    (mf, maskf), (mb, maskb) = consts
    n_rows = mf.shape[0]
    n_lev = maskf.shape[0]

    def fwd_block(b, s):
        return jnp.where(s < n_ctx, ctx_base + b * n_ctx + s, b * n_lat + (s - n_ctx))

    def bwd_block(b, s):
        return jnp.where(s < n_ctx, ctx_base + b * n_ctx + (n_ctx - 1 - s),
                         b * n_lat + (n_lat - 1 - (s - n_ctx)))

    def spec(block_fn, colblk):
        return pl.BlockSpec((LB, bw), lambda b, s: (block_fn(b, s), colblk))

    const = lambda shape: pl.BlockSpec(shape, lambda b, s: (0,) * len(shape))
    return pl.pallas_call(
        _hgrn_kernel,
        grid=(n_batch, steps),
        in_specs=[spec(fwd_block, col0), spec(fwd_block, col0 + 1), spec(fwd_block, col0 + 3),
                  spec(bwd_block, col0), spec(bwd_block, col0 + 2), spec(bwd_block, col0 + 3),
                  const((2, bw)), const((n_rows, LB)), const((n_rows, LB)),
                  const((n_lev, LB, LB)), const((n_lev, LB, LB))],
        out_specs=[spec(fwd_block, 0), spec(bwd_block, 0)],
        out_shape=[jax.ShapeDtypeStruct((t, bw), F32)] * 2,
        scratch_shapes=[pltpu.VMEM((2 * (bw // B_DK), B_DK, B_DK), F32),
                        pltpu.VMEM((n_rows, bw), F32)],
        compiler_params=_cparams(("parallel", "arbitrary")),
        name="hgrn",
    )(p, p, p, p, p, p, lb, mf, mb, maskf, maskb)


def _scan_rows(a, h, reverse):
    n = a.shape[0]
    idx = lax.broadcasted_iota(jnp.int32, (n, 1), 0)
    k = 1
    while k < n:
        if reverse:
            a_s, h_s, valid = pltpu.roll(a, n - k, 0), pltpu.roll(h, n - k, 0), idx < n - k
        else:
            a_s, h_s, valid = pltpu.roll(a, k, 0), pltpu.roll(h, k, 0), idx >= k
        h = jnp.where(valid, a * h_s + h, h)
        a = jnp.where(valid, a_s * a, a)
        k *= 2
    return a, h


def _shift_rows(x, delta):
    n = x.shape[0]
    idx = lax.broadcasted_iota(jnp.int32, (n, 1), 0)
    if delta == 0:
        return x
    y = pltpu.roll(x, (-delta) % n, 0)
    valid = (idx + delta >= 0) & (idx + delta < n)
    return jnp.where(valid, y, 0.0)


def _rg_gates(y, wg, gb, sp, d):
    c = y.shape[1]
    g = _dot(y.astype(BF16), wg[:, 2 * d * c:(2 * d + 2) * c]) + gb[:, 2 * d * c:(2 * d + 2) * c]
    r = _sigmoid(g[:, :c])
    i = _sigmoid(g[:, c:])
    log_a = (-RG_C) * r * sp[d:d + 1, :]
    a = jnp.exp(log_a)
    return a, jnp.sqrt(1.0 - a * a) * (i * y)


def _rglru_kernel(xl_ref, xc_ref, cw_ref, cb_ref, wg_ref, gb_ref, lam_ref, ol_ref, oc_ref,
                  xe_ref, a_ref, bx_ref, *, n_rows):
    w = GRID_W
    seq, c = xl_ref.shape
    sp = _softplus(-lam_ref[...])
    cw = cw_ref[...]
    cb = cb_ref[...]
    wg = wg_ref[...]
    gb = gb_ref[...]

    xc = xc_ref[...]
    yc = cb + sum(cw[j:j + 1, :] * _shift_rows(xc, j - 2) for j in range(CONV_W))
    h0 = []
    for d in range(2):
        a, bx = _rg_gates(yc, wg, gb, sp, d)
        _, hs = _scan_rows(a, bx, reverse=(d == 1))
        n_ctx = hs.shape[0]
        h0.append(hs[n_ctx - 1:n_ctx, :] if d == 0 else hs[0:1, :])
        if d == 0:
            oc_ref[...] = hs
        else:
            oc_ref[...] += hs

    xe_ref[2 * w:2 * w + seq, :] = xl_ref[...]
    xe_ref[0:w, :] = _shift_rows(xl_ref[(n_rows - 2) * w:(n_rows - 1) * w, :], -1)
    xe_ref[w:2 * w, :] = _shift_rows(xl_ref[(n_rows - 1) * w:n_rows * w, :], -1)
    xe_ref[2 * w + seq:3 * w + seq, :] = _shift_rows(xl_ref[0:w, :], 1)

    for d in range(2):
        def gate_chunk(i, carry):
            base = pl.multiple_of(i * RG_ROWS, RG_ROWS)
            y = cb + sum(cw[j:j + 1, :] * xe_ref[pl.ds(base + j * w, RG_ROWS), :] for j in range(CONV_W))
            a, bx = _rg_gates(y, wg, gb, sp, d)
            a_ref[pl.ds(base, RG_ROWS), :] = a
            bx_ref[pl.ds(base, RG_ROWS), :] = bx
            return carry

        lax.fori_loop(0, seq // RG_ROWS, gate_chunk, 0)

        def slab(i):
            r = i if d == 0 else n_rows - 1 - i
            return pl.ds(pl.multiple_of(r * w, w), w)

        def column_totals(i, carry):
            a_tot, h_end = carry
            a = a_ref[slab(i), :]
            return a * a_tot, a * h_end + bx_ref[slab(i), :]

        a_tot, h_end = lax.fori_loop(0, n_rows, column_totals,
                                     (jnp.ones((w, c), F32), jnp.zeros((w, c), F32)))
        a_cum, h_cum = _scan_rows(a_tot, h_end, reverse=(d == 1))
        after = a_cum * h0[d] + h_cum
        if d == 0:
            h_in = jnp.where(lax.broadcasted_iota(jnp.int32, (w, 1), 0) == 0, h0[d], pltpu.roll(after, 1, 0))
        else:
            h_in = jnp.where(lax.broadcasted_iota(jnp.int32, (w, 1), 0) == w - 1, h0[d],
                             pltpu.roll(after, w - 1, 0))

        def emit(i, h):
            h = a_ref[slab(i), :] * h + bx_ref[slab(i), :]
            if d == 0:
                ol_ref[slab(i), :] = h
            else:
                ol_ref[slab(i), :] += h
            return h

        lax.fori_loop(0, n_rows, emit, h_in)


def _rglru(p, conv_w, conv_b, wg, gb, lam, n_batch, seq, ctx_len, col0):
    cw_total = conv_w.shape[1]
    n_slabs = cw_total // RG_SLAB
    n_rows = seq // GRID_W
    ctx_base = n_batch * seq // ctx_len
    return pl.pallas_call(
        functools.partial(_rglru_kernel, n_rows=n_rows),
        grid=(n_batch, n_slabs),
        in_specs=[
            pl.BlockSpec((seq, RG_SLAB), lambda b, j: (b, col0 * n_slabs + j)),
            pl.BlockSpec((ctx_len, RG_SLAB), lambda b, j: (ctx_base + b, col0 * n_slabs + j)),
            pl.BlockSpec((CONV_W, RG_SLAB), lambda b, j: (0, j)),
            pl.BlockSpec((1, RG_SLAB), lambda b, j: (0, j)),
            pl.BlockSpec((None, RG_SLAB, 4 * RG_SLAB), lambda b, j: (j, 0, 0)),
            pl.BlockSpec((None, 1, 4 * RG_SLAB), lambda b, j: (j, 0, 0)),
            pl.BlockSpec((2, RG_SLAB), lambda b, j: (0, j)),
        ],
        out_specs=[pl.BlockSpec((seq, RG_SLAB), lambda b, j: (b, j)),
                   pl.BlockSpec((ctx_len, RG_SLAB), lambda b, j: (b, j))],
        out_shape=[jax.ShapeDtypeStruct((n_batch * seq, cw_total), F32),
                   jax.ShapeDtypeStruct((n_batch * ctx_len, cw_total), F32)],
        scratch_shapes=[pltpu.VMEM((seq + 3 * GRID_W, RG_SLAB), F32),
                        pltpu.VMEM((seq, RG_SLAB), F32), pltpu.VMEM((seq, RG_SLAB), F32)],
        compiler_params=_cparams(("parallel", "arbitrary")),
        name="rglru",
    )(p, p, conv_w, conv_b, wg, gb, lam)


def _head_rms(x, width):
    parts = []
    for i in range(x.shape[1] // width):
        seg = x[:, i * width:(i + 1) * width]
        parts.append(seg * lax.rsqrt(jnp.mean(seg * seg, axis=-1, keepdims=True) + EPS))
    return jnp.concatenate(parts, axis=-1)


def _merge_kernel(af_ref, ab_ref, aog_ref, bf_ref, bb_ref, bog_ref, yc_ref, cg_ref,
                  ga_ref, gb_ref, gc_ref, h_ref, mod_ref, bw_ref, wo_ref, o_ref,
                  ya_ref, yb_ref, ycs_ref, acc_ref, *, mod_base, dh):
    j = pl.program_id(1)

    @pl.when(j == 0)
    def _():
        ya_ref[...] = (_head_rms(af_ref[...] + ab_ref[...], dh) * _sigmoid(aog_ref[...])).astype(BF16)
        og = bog_ref[...]
        yb_ref[...] = (_head_rms(bf_ref[...] + bb_ref[...], B_DK) * (og * _sigmoid(og))).astype(BF16)
        ycs_ref[...] = (yc_ref[...] * _gelu_tanh(cg_ref[...])).astype(BF16)
        acc_ref[...] = jnp.zeros_like(acc_ref)

    merged = (_sigmoid(ga_ref[...]) * _dot(ya_ref[...], bw_ref[0])
              + _sigmoid(gb_ref[...]) * _dot(yb_ref[...], bw_ref[1])
              + _sigmoid(gc_ref[...]) * _dot(ycs_ref[...], bw_ref[2]))
    acc_ref[...] += _dot(merged.astype(BF16), wo_ref[...])

    @pl.when(j == pl.num_programs(1) - 1)
    def _():
        o_ref[...] = h_ref[...] + mod_ref[mod_base:mod_base + 1, :] * acc_ref[...]


def _merge(h, n_rows, p, a_f, a_b, b_f, b_b, yc, modtab, mod_row, branch_w, w_out, layer, mod_base, dh):
    d = h.shape[1]
    bw = a_f.shape[1]
    n_j = d // MERGE_TILE
    gate0 = 11 * bw // MERGE_TILE
    per_branch = d // MERGE_TILE
    row = lambda colblk: pl.BlockSpec((MERGE_ROWS, bw), lambda i, j: (i, colblk))
    gate = lambda k: pl.BlockSpec((MERGE_ROWS, MERGE_TILE), lambda i, j: (i, gate0 + k * per_branch + j))
    return pl.pallas_call(
        functools.partial(_merge_kernel, mod_base=mod_base, dh=dh),
        grid=(n_rows // MERGE_ROWS, n_j),
        in_specs=[row(0), row(0), row(3), row(0), row(0), row(8), row(0), row(10),
                  gate(0), gate(1), gate(2),
                  pl.BlockSpec((MERGE_ROWS, d), lambda i, j: (i, 0)),
                  pl.BlockSpec((None, N_MOD, d), lambda i, j: (mod_row(MERGE_ROWS)(i), 0, 0)),
                  pl.BlockSpec((None, 3, bw, MERGE_TILE), lambda i, j: (layer, 0, 0, j)),
                  pl.BlockSpec((None, MERGE_TILE, d), lambda i, j: (layer, j, 0))],
        out_specs=pl.BlockSpec((MERGE_ROWS, d), lambda i, j: (i, 0)),
        out_shape=jax.ShapeDtypeStruct((n_rows, d), F32),
        scratch_shapes=[pltpu.VMEM((MERGE_ROWS, bw), BF16)] * 3 + [pltpu.VMEM((MERGE_ROWS, d), F32)],
        compiler_params=_cparams(("parallel", "arbitrary")),
        name="merge",
    )(a_f, a_b, p, b_f, b_b, p, yc, p, p, p, p, h, modtab, branch_w, w_out)


def kernel(x, c, ctx, c_ctx, mod_w, mod_b, norm_g, ffn_w_in, ffn_w_out, w_in, mlstm_gate_b,
           hgrn_lb_logits, conv_w, conv_b, rg_gate_w, rg_gate_b, rg_lambda, branch_w, w_out, final_g):
    n_batch, seq, d = x.shape
    ctx_len = ctx.shape[1]
    depth = mod_w.shape[0]
    aw = branch_w.shape[2]
    dh = aw // A_HEADS
    n_lat_rows = n_batch * seq
    n_rows = n_lat_rows + n_batch * ctx_len
    assert seq % ROW_TILE == 0 and (n_batch * ctx_len) % ROW_TILE == 0
    assert seq % LA == 0 and ctx_len % LA == 0 and seq % LB == 0 and ctx_len % LB == 0
    assert seq % GRID_W == 0 and seq % RG_ROWS == 0 and n_lat_rows % ctx_len == 0
    assert n_batch + 1 <= MOD_ROWS and aw % RG_SLAB == 0 and RG_SLAB % (aw // C_BLOCKS) == 0

    def mod_row(tile):
        return lambda i: jnp.where(i < n_lat_rows // tile, 1 + i // (seq // tile), 0)

    c16 = jnp.zeros((MOD_ROWS, d), F32).at[0].set(c_ctx).at[1:1 + n_batch].set(c)
    modtab = _mod_table(c16, mod_w, mod_b).reshape(depth, MOD_ROWS, N_MOD, d)
    ffn_w_in_b = ffn_w_in.astype(BF16)
    ffn_w_out_b = ffn_w_out.astype(BF16)
    g0 = 4 * aw
    n_gate = 4 * A_HEADS
    w_main = jnp.concatenate([w_in[:, :, :g0], w_in[:, :, g0 + n_gate:]], axis=-1).astype(BF16)
    w_gate = jnp.pad(w_in[:, :, g0:g0 + n_gate], ((0, 0), (0, 0), (0, GATE_PAD - n_gate))).astype(BF16)
    gate_bias = jnp.pad(mlstm_gate_b.reshape(depth, 1, n_gate), ((0, 0), (0, 0), (0, GATE_PAD - n_gate)))
    lb_p = jax.nn.softmax(hgrn_lb_logits.astype(F32), axis=0)
    lb_all = jnp.cumsum(lb_p, axis=0) - lb_p[0:1]
    branch_w_b = branch_w.astype(BF16)
    w_out_b = w_out.astype(BF16)
    n_slabs = aw // RG_SLAB
    c_db = aw // C_BLOCKS
    per_slab = RG_SLAB // c_db
    blocks = rg_gate_w.reshape(depth, 2, 2, n_slabs, per_slab, c_db, c_db)
    eye = jnp.eye(per_slab, dtype=F32)
    dense = jnp.einsum('ldgspio,pq->ldgspiqo', blocks, eye).reshape(depth, 2, 2, n_slabs, RG_SLAB, RG_SLAB)
    rg_w = dense.transpose(0, 3, 4, 1, 2, 5).reshape(depth, n_slabs, RG_SLAB, 4 * RG_SLAB).astype(BF16)
    rg_b = (rg_gate_b.reshape(depth, 2, 2, n_slabs, RG_SLAB).transpose(0, 3, 1, 2, 4)
            .reshape(depth, n_slabs, 1, 4 * RG_SLAB))
    tri = np.tril(np.ones((LA, LA), np.float32))
    tri_f = jnp.asarray(tri, BF16)
    tri_b = jnp.asarray(tri.T, BF16)
    hg = _hgrn_constants(LB)
    hgrn_consts = tuple((jnp.asarray(m, BF16), jnp.asarray(k, F32)) for m, k in hg)

    h = jnp.concatenate([x.reshape(n_lat_rows, d), ctx.reshape(n_batch * ctx_len, d)], axis=0)
    for layer in range(depth):
        last = layer == depth - 1
        mt = modtab[layer]
        ng = norm_g[layer]
        h = _ffn(h, n_rows, mt, mod_row, ng[0:1], ffn_w_in_b, ffn_w_out_b, layer, 0, 0)
        p, gates = _inproj(h, mt, mod_row, ng[1:2], w_main, w_gate, layer, 3)
        a_f, a_b = _mlstm(p, gates, gate_bias[layer], tri_f, tri_b, n_batch, seq, ctx_len, dh)
        b_f, b_b = _hgrn(p, lb_all[layer], hgrn_consts, n_batch, seq, ctx_len, 4)
        yc_lat, yc_ctx = _rglru(p, conv_w[layer], conv_b[layer][None, :], rg_w[layer], rg_b[layer],
                                rg_lambda[layer], n_batch, seq, ctx_len, 9)
        yc = jnp.concatenate([yc_lat, yc_ctx], axis=0)
        rows_out = n_lat_rows if last else n_rows
        h = _merge(h, rows_out, p, a_f, a_b, b_f, b_b, yc, mt, mod_row, branch_w_b, w_out_b, layer, 5, dh)
        h = _ffn(h, rows_out, mt, mod_row, ng[2:3], ffn_w_in_b, ffn_w_out_b, layer, 1, 6,
                 final_g=final_g[None, :] if last else None)
    return h.reshape(n_batch, seq, d)
```

```python
import functools

import numpy as np
import jax
import jax.numpy as jnp
from jax import lax
from jax.experimental import pallas as pl
from jax.experimental.pallas import tpu as pltpu

EPS = 1e-6
N_MOD = 9
GRID_W = 64
A_HEADS = 4
B_DK = 128
C_BLOCKS = 16
CONV_W = 4
RG_C = 8.0
NEG_BIG = -1e30

MOD_ROWS = 16
GATE_PAD = 128
ROW_TILE = 1024
FF_TILE = 512
FFN_SUB = 512
PROJ_TILE = 1024
MOD_TILE = 1024
MERGE_TILE = 512
MERGE_ROWS = 512
LA = 256
LB = 128
RG_SLAB = 256
RG_ROWS = 512
VMEM_LIMIT = 60 * 1024 * 1024

F32 = jnp.float32
BF16 = jnp.bfloat16


def _cparams(sem):
    return pltpu.CompilerParams(dimension_semantics=sem, vmem_limit_bytes=VMEM_LIMIT)


def _dot(a, b):
    return jnp.dot(a, b, preferred_element_type=F32)


def _dot_nt(a, b):
    return lax.dot_general(a, b, (((1,), (1,)), ((), ())), preferred_element_type=F32)


def _dot01(m3, x):
    hi = x.astype(BF16)
    r1 = x - hi.astype(F32)
    mid = r1.astype(BF16)
    lo = (r1 - mid.astype(F32)).astype(BF16)
    return _dot(m3, jnp.concatenate([hi, mid, lo], axis=0))


def _sigmoid(x):
    return jax.nn.sigmoid(x)


def _softplus(z):
    return jnp.maximum(z, 0.0) + jnp.log1p(jnp.exp(-jnp.abs(z)))


def _gelu_tanh(x):
    return 0.5 * x * (1.0 + jnp.tanh(np.sqrt(2.0 / np.pi).astype(np.float32) * (x + 0.044715 * (x * x * x))))


def _rms(x):
    return x * lax.rsqrt(jnp.mean(x * x, axis=-1, keepdims=True) + EPS)


def _adaln(h, g, shift, scale):
    return (_rms(h) * g) * (1.0 + scale) + shift


def _mod_kernel(c_ref, w_ref, b_ref, o_ref):
    cv = c_ref[...]
    sc = (cv * _sigmoid(cv)).astype(BF16)
    o_ref[...] = _dot(sc, w_ref[...].astype(BF16)) + b_ref[...]


def _mod_table(c16, mod_w, mod_b):
    depth, d, n = mod_w.shape
    return pl.pallas_call(
        _mod_kernel,
        grid=(depth, n // MOD_TILE),
        in_specs=[
            pl.BlockSpec((MOD_ROWS, d), lambda l, j: (0, 0)),
            pl.BlockSpec((None, d, MOD_TILE), lambda l, j: (l, 0, j)),
            pl.BlockSpec((None, 1, MOD_TILE), lambda l, j: (l, 0, j)),
        ],
        out_specs=pl.BlockSpec((None, MOD_ROWS, MOD_TILE), lambda l, j: (l, 0, j)),
        out_shape=jax.ShapeDtypeStruct((depth, MOD_ROWS, n), F32),
        compiler_params=_cparams(("arbitrary", "arbitrary")),
        name="mod_table",
    )(c16, mod_w, mod_b.reshape(depth, 1, n))


def _ffn_kernel(h_ref, mod_ref, g_ref, wg_ref, wu_ref, wo_ref, *rest, mod_base, final):
    if final:
        fg_ref, o_ref, xn_ref = rest
    else:
        o_ref, xn_ref = rest
    j = pl.program_id(1)
    subs = [slice(r, r + FFN_SUB) for r in range(0, h_ref.shape[0], FFN_SUB)]

    @pl.when(j == 0)
    def _():
        for rows in subs:
            xn = _adaln(h_ref[rows, :], g_ref[...], mod_ref[mod_base:mod_base + 1, :],
                        mod_ref[mod_base + 1:mod_base + 2, :])
            xn_ref[rows, :] = xn.astype(BF16)
            o_ref[rows, :] = jnp.zeros((FFN_SUB, o_ref.shape[1]), F32)

    for rows in subs:
        xn = xn_ref[rows, :]
        gate = _dot(xn, wg_ref[...])
        up = _dot(xn, wu_ref[...])
        act = (gate * _sigmoid(gate)) * up
        o_ref[rows, :] += _dot(act.astype(BF16), wo_ref[...])

    @pl.when(j == pl.num_programs(1) - 1)
    def _():
        for rows in subs:
            out = h_ref[rows, :] + (0.5 * mod_ref[mod_base + 2:mod_base + 3, :]) * o_ref[rows, :]
            if final:
                out = _rms(out) * fg_ref[...]
            o_ref[rows, :] = out


def _ffn(h, n_rows, modtab, mod_row, g, w_in, w_out, layer, which, mod_base, final_g=None):
    d = h.shape[1]
    d_ff = w_out.shape[2]
    n_ff = d_ff // FF_TILE
    final = final_g is not None
    in_specs = [
        pl.BlockSpec((ROW_TILE, d), lambda i, j: (i, 0)),
        pl.BlockSpec((None, N_MOD, d), lambda i, j: (mod_row(ROW_TILE)(i), 0, 0)),
        pl.BlockSpec((1, d), lambda i, j: (0, 0)),
        pl.BlockSpec((None, None, d, FF_TILE), lambda i, j: (layer, which, 0, j)),
        pl.BlockSpec((None, None, d, FF_TILE), lambda i, j: (layer, which, 0, j + n_ff)),
        pl.BlockSpec((None, None, FF_TILE, d), lambda i, j: (layer, which, j, 0)),
    ]
    args = [h, modtab, g, w_in, w_in, w_out]
    if final:
        in_specs.append(pl.BlockSpec((1, d), lambda i, j: (0, 0)))
        args.append(final_g)
    return pl.pallas_call(
        functools.partial(_ffn_kernel, mod_base=mod_base, final=final),
        grid=(n_rows // ROW_TILE, n_ff),
        in_specs=in_specs,
        out_specs=pl.BlockSpec((ROW_TILE, d), lambda i, j: (i, 0)),
        out_shape=jax.ShapeDtypeStruct((n_rows, d), F32),
        scratch_shapes=[pltpu.VMEM((ROW_TILE, d), BF16)],
        compiler_params=_cparams(("parallel", "arbitrary")),
        name="ffn",
    )(*args)


def _inproj_kernel(h_ref, mod_ref, g_ref, w_ref, wgate_ref, p_ref, gate_ref, xn_ref, *, mod_base):
    @pl.when(pl.program_id(1) == 0)
    def _():
        xn = _adaln(h_ref[...], g_ref[...], mod_ref[mod_base:mod_base + 1, :],
                    mod_ref[mod_base + 1:mod_base + 2, :]).astype(BF16)
        xn_ref[...] = xn
        gate_ref[...] = _dot(xn, wgate_ref[...])

    p_ref[...] = _dot(xn_ref[...], w_ref[...])


def _inproj(h, modtab, mod_row, g, w_main, w_gate, layer, mod_base):
    t, d = h.shape
    n = w_main.shape[2]
    return pl.pallas_call(
        functools.partial(_inproj_kernel, mod_base=mod_base),
        grid=(t // ROW_TILE, n // PROJ_TILE),
        in_specs=[
            pl.BlockSpec((ROW_TILE, d), lambda i, j: (i, 0)),
            pl.BlockSpec((None, N_MOD, d), lambda i, j: (mod_row(ROW_TILE)(i), 0, 0)),
            pl.BlockSpec((1, d), lambda i, j: (0, 0)),
            pl.BlockSpec((None, d, PROJ_TILE), lambda i, j: (layer, 0, j)),
            pl.BlockSpec((None, d, GATE_PAD), lambda i, j: (layer, 0, 0)),
        ],
        out_specs=[
            pl.BlockSpec((ROW_TILE, PROJ_TILE), lambda i, j: (i, j)),
            pl.BlockSpec((ROW_TILE, GATE_PAD), lambda i, j: (i, 0)),
        ],
        out_shape=[jax.ShapeDtypeStruct((t, n), F32), jax.ShapeDtypeStruct((t, GATE_PAD), F32)],
        scratch_shapes=[pltpu.VMEM((ROW_TILE, d), BF16)],
        compiler_params=_cparams(("parallel", "arbitrary")),
        name="inproj",
    )(h, modtab, g, w_main, w_gate)


def _chunk_block(n_batch, seq, ctx_len, length, direction):
    n_lat = seq // length
    n_ctx = ctx_len // length
    ctx_base = n_batch * seq // length

    def block(b, s):
        if direction == 0:
            return jnp.where(s < n_ctx, ctx_base + b * n_ctx + s, b * n_lat + (s - n_ctx))
        return jnp.where(s < n_ctx, ctx_base + b * n_ctx + (n_ctx - 1 - s),
                         b * n_lat + (n_lat - 1 - (s - n_ctx)))

    return block, n_ctx + n_lat


def _mlstm_kernel(q_ref, k_ref, v_ref, g_ref, *rest, dh, direction, final):
    if final:
        of_ref, og_ref, bias_ref, tri_ref, o_ref, c_ref, n_ref, m_ref = rest
    else:
        bias_ref, tri_ref, o_ref, c_ref, n_ref, m_ref = rest
    length = q_ref.shape[0]

    @pl.when(pl.program_id(1) == 0)
    def _():
        c_ref[...] = jnp.zeros_like(c_ref)
        n_ref[...] = jnp.zeros_like(n_ref)
        m_ref[...] = jnp.zeros_like(m_ref)

    row = lax.broadcasted_iota(jnp.int32, (length, length), 0)
    col = lax.broadcasted_iota(jnp.int32, (length, length), 1)
    mask = (col <= row) if direction == 0 else (col >= row)
    last = length - 1 if direction == 0 else 0
    pre = g_ref[...] + bias_ref[...]
    logf = jnp.minimum(pre, 0.0) - jnp.log1p(jnp.exp(-jnp.abs(pre)))
    b_all = _dot01(tri_ref[...], logf)
    b_all_t = b_all.T
    pre_t = pre.T
    for hh in range(A_HEADS):
        ci = direction * 2 * A_HEADS + hh
        cf = ci + A_HEADS
        sl = slice(hh * dh, (hh + 1) * dh)
        b_col = b_all[:, cf:cf + 1]
        b_row = b_all_t[cf:cf + 1, :]
        i_col = pre[:, ci:ci + 1]
        i_row = pre_t[ci:ci + 1, :]
        m_prev = m_ref[hh][:, 0:1]
        cmat = c_ref[hh]
        nvec = n_ref[hh]

        dmat = jnp.where(mask, b_col - b_row + i_row, NEG_BIG)
        inter = b_col + m_prev
        m_t = jnp.maximum(inter, jnp.max(dmat, axis=-1, keepdims=True))
        w_inter = jnp.exp(inter - m_t)
        qf = q_ref[:, sl]
        kf = k_ref[:, sl] * (dh ** -0.5)
        q = qf.astype(BF16)
        v = v_ref[:, sl].astype(BF16)
        s = _dot_nt(q, kf.astype(BF16)) * jnp.exp(dmat - m_t)
        num = w_inter * _dot(q, cmat.astype(BF16)) + _dot(s.astype(BF16), v)
        den = (w_inter * jnp.sum(qf * nvec, axis=-1, keepdims=True)
               + jnp.sum(s, axis=-1, keepdims=True))
        out = num / jnp.maximum(jnp.abs(den), jnp.exp(-m_t))
        if final:
            o_ref[:, sl] = (_rms(out + of_ref[:, sl]) * _sigmoid(og_ref[:, sl])).astype(BF16)
        else:
            o_ref[:, sl] = out

        b_last = b_all[last:last + 1, cf:cf + 1]
        wlog = b_last - b_col + i_col
        m_new = jnp.maximum(b_last + m_prev, jnp.max(wlog, axis=0, keepdims=True))
        decay = jnp.exp(b_last + m_prev - m_new)
        kw = kf * jnp.exp(wlog - m_new)
        c_ref[hh] = decay * cmat + _dot(kw.T.astype(BF16), v)
        n_ref[hh] = decay * nvec + jnp.sum(kw, axis=0, keepdims=True)
        m_ref[hh] = jnp.broadcast_to(m_new, m_ref.shape[1:])


def _mlstm(p, gates, gate_bias, tri3, n_batch, seq, ctx_len, dh, direction, o_fwd=None):
    t = p.shape[0]
    aw = A_HEADS * dh
    block, steps = _chunk_block(n_batch, seq, ctx_len, LA, direction)
    final = o_fwd is not None
    spec = lambda colblk, width: pl.BlockSpec((LA, width), lambda b, s: (block(b, s), colblk))
    const = lambda shape: pl.BlockSpec(shape, lambda b, s: (0,) * len(shape))
    in_specs = [spec(0, aw), spec(1, aw), spec(2, aw), spec(0, GATE_PAD)]
    args = [p, p, p, gates]
    if final:
        in_specs += [spec(0, aw), spec(3, aw)]
        args += [o_fwd, p]
    in_specs += [const((1, GATE_PAD)), const((LA, 3 * LA))]
    args += [gate_bias, tri3]
    return pl.pallas_call(
        functools.partial(_mlstm_kernel, dh=dh, direction=direction, final=final),
        grid=(n_batch, steps),
        in_specs=in_specs,
        out_specs=spec(0, aw),
        out_shape=jax.ShapeDtypeStruct((t, aw), BF16 if final else F32),
        scratch_shapes=[pltpu.VMEM((A_HEADS, dh, dh), F32),
                        pltpu.VMEM((A_HEADS, 1, dh), F32),
                        pltpu.VMEM((A_HEADS, 1, 128), F32)],
        compiler_params=_cparams(("parallel", "arbitrary")),
        name="mlstm_bwd" if final else "mlstm_fwd",
    )(*args)


def _hgrn_levels(length):
    levels = []
    c = length // 2
    while c >= 1:
        levels.append(c)
        c //= 2
    return levels


def _hgrn_constants(length, direction):
    t = np.arange(length)[:, None]
    u = np.arange(length)[None, :]
    if direction == 0:
        mats = [u <= t, u > t]
    else:
        mats = [u >= t, u < t]
    masks = []
    for c in _hgrn_levels(length):
        blk = t // (2 * c)
        pos = t % (2 * c)
        if direction == 0:
            ref = blk * 2 * c + c - 1
            q_role = pos >= c
            m = np.where(q_role, (u > ref) & (u <= t), (u > t) & (u <= ref))
            keep = (blk == blk.T) & q_role & (pos.T < c)
        else:
            ref = blk * 2 * c + c
            q_role = pos < c
            m = np.where(q_role, (u >= t) & (u < ref), (u >= ref) & (u < t))
            keep = (blk == blk.T) & q_role & (pos.T >= c)
        mats.append(m)
        masks.append(keep)
    m_all = np.concatenate(mats, 0).astype(np.float32)
    return np.concatenate([m_all] * 3, axis=1), np.stack(masks).astype(np.float32)


def _hgrn_kernel(q_ref, f_ref, v_ref, *rest, direction, final):
    if final:
        of_ref, og_ref, lb_ref, m3_ref, mask_ref, o_ref, st_ref, d_ref = rest
    else:
        lb_ref, m3_ref, mask_ref, o_ref, st_ref, d_ref = rest
    length, width = q_ref.shape
    n_heads = width // B_DK
    levels = _hgrn_levels(length)
    last = length - 1 if direction == 0 else 0

    @pl.when(pl.program_id(1) == 0)
    def _():
        st_ref[...] = jnp.zeros_like(st_ref)

    t_idx = lax.broadcasted_iota(jnp.int32, (length, 1), 0)
    eye = (lax.broadcasted_iota(jnp.int32, (length, length), 0)
           == lax.broadcasted_iota(jnp.int32, (length, length), 1))
    lb = lb_ref[...]
    f = lb + (1.0 - lb) * _sigmoid(f_ref[...])
    kk = 1.0 - f
    qpre = q_ref[...]
    q = qpre * _sigmoid(qpre)
    d_ref[...] = _dot01(m3_ref[...], jnp.log(f))
    q_in = (q * jnp.exp(d_ref[0:length, :])).astype(BF16)
    k_st = kk * jnp.exp(d_ref[length:2 * length, :])
    decay = jnp.exp(d_ref[last:last + 1, :])
    xs = []
    for li, c in enumerate(levels):
        pos = t_idx % (2 * c)
        q_role = (pos >= c) if direction == 0 else (pos < c)
        e = jnp.exp(d_ref[(2 + li) * length:(3 + li) * length, :])
        xs.append((jnp.where(q_role, q, kk) * e).astype(BF16))
    qk = q * kk
    v_all = v_ref[...]
    for hd in range(n_heads):
        sl = slice(hd * B_DK, (hd + 1) * B_DK)
        att = jnp.where(eye, jnp.sum(qk[:, sl], axis=-1, keepdims=True), 0.0)
        for li in range(len(levels)):
            x = xs[li][:, sl]
            att = att + mask_ref[li] * _dot_nt(x, x)
        vh = v_all[:, sl]
        st = st_ref[hd]
        out = _dot(att.astype(BF16), vh.astype(BF16)) + _dot_nt(q_in[:, sl], st.astype(BF16))
        if final:
            og = og_ref[:, sl]
            o_ref[:, sl] = (_rms(out + of_ref[:, sl]) * (og * _sigmoid(og))).astype(BF16)
        else:
            o_ref[:, sl] = out
        st_ref[hd] = st * decay[:, sl] + _dot(vh.T.astype(BF16), k_st[:, sl].astype(BF16))


def _hgrn(p, lb, consts, n_batch, seq, ctx_len, col0, direction, o_fwd=None):
    t = p.shape[0]
    bw = lb.shape[1]
    m3, masks = consts
    n_rows = m3.shape[0]
    n_lev = masks.shape[0]
    block, steps = _chunk_block(n_batch, seq, ctx_len, LB, direction)
    final = o_fwd is not None
    spec = lambda colblk: pl.BlockSpec((LB, bw), lambda b, s: (block(b, s), colblk))
    const = lambda shape: pl.BlockSpec(shape, lambda b, s: (0,) * len(shape))
    in_specs = [spec(col0), spec(col0 + 1 + direction), spec(col0 + 3)]
    args = [p, p, p]
    if final:
        in_specs += [spec(0), spec(col0 + 4)]
        args += [o_fwd, p]
    in_specs += [const((1, bw)), const((n_rows, 3 * LB)), const((n_lev, LB, LB))]
    args += [lb, m3, masks]
    return pl.pallas_call(
        functools.partial(_hgrn_kernel, direction=direction, final=final),
        grid=(n_batch, steps),
        in_specs=in_specs,
        out_specs=spec(0),
        out_shape=jax.ShapeDtypeStruct((t, bw), BF16 if final else F32),
        scratch_shapes=[pltpu.VMEM((bw // B_DK, B_DK, B_DK), F32),
                        pltpu.VMEM((n_rows, bw), F32)],
        compiler_params=_cparams(("parallel", "arbitrary")),
        name="hgrn_bwd" if final else "hgrn_fwd",
    )(*args)


def _scan_rows(a, h, reverse):
    n = a.shape[0]
    idx = lax.broadcasted_iota(jnp.int32, (n, 1), 0)
    k = 1
    while k < n:
        if reverse:
            a_s, h_s, valid = pltpu.roll(a, n - k, 0), pltpu.roll(h, n - k, 0), idx < n - k
        else:
            a_s, h_s, valid = pltpu.roll(a, k, 0), pltpu.roll(h, k, 0), idx >= k
        h = jnp.where(valid, a * h_s + h, h)
        a = jnp.where(valid, a_s * a, a)
        k *= 2
    return a, h


def _shift_rows(x, delta):
    n = x.shape[0]
    idx = lax.broadcasted_iota(jnp.int32, (n, 1), 0)
    if delta == 0:
        return x
    y = pltpu.roll(x, (-delta) % n, 0)
    valid = (idx + delta >= 0) & (idx + delta < n)
    return jnp.where(valid, y, 0.0)


def _rg_gates(y, wg, gb, sp, d):
    c = y.shape[1]
    g = _dot(y.astype(BF16), wg[:, 2 * d * c:(2 * d + 2) * c]) + gb[:, 2 * d * c:(2 * d + 2) * c]
    r = _sigmoid(g[:, :c])
    i = _sigmoid(g[:, c:])
    log_a = (-RG_C) * r * sp[d:d + 1, :]
    a = jnp.exp(log_a)
    return a, jnp.sqrt(1.0 - a * a) * (i * y)


def _rglru_kernel(xl_ref, xc_ref, gl_ref, gc_ref, cw_ref, cb_ref, wg_ref, gb_ref, lam_ref,
                  ol_ref, oc_ref, xe_ref, a_ref, bx_ref, hf_ref, *, n_rows):
    w = GRID_W
    seq, c = xl_ref.shape
    sp = _softplus(-lam_ref[...])
    cw = cw_ref[...]
    cb = cb_ref[...]
    wg = wg_ref[...]
    gb = gb_ref[...]

    xc = xc_ref[...]
    yc = cb + sum(cw[j:j + 1, :] * _shift_rows(xc, j - 2) for j in range(CONV_W))
    a, bx = _rg_gates(yc, wg, gb, sp, 0)
    _, hs_f = _scan_rows(a, bx, reverse=False)
    a, bx = _rg_gates(yc, wg, gb, sp, 1)
    _, hs_b = _scan_rows(a, bx, reverse=True)
    n_ctx = xc.shape[0]
    h0 = (hs_f[n_ctx - 1:n_ctx, :], hs_b[0:1, :])
    oc_ref[...] = ((hs_f + hs_b) * _gelu_tanh(gc_ref[...])).astype(BF16)

    xe_ref[2 * w:2 * w + seq, :] = xl_ref[...]
    xe_ref[0:w, :] = _shift_rows(xl_ref[(n_rows - 2) * w:(n_rows - 1) * w, :], -1)
    xe_ref[w:2 * w, :] = _shift_rows(xl_ref[(n_rows - 1) * w:n_rows * w, :], -1)
    xe_ref[2 * w + seq:3 * w + seq, :] = _shift_rows(xl_ref[0:w, :], 1)

    for d in range(2):
        def gate_chunk(i, carry):
            base = pl.multiple_of(i * RG_ROWS, RG_ROWS)
            y = cb + sum(cw[j:j + 1, :] * xe_ref[pl.ds(base + j * w, RG_ROWS), :] for j in range(CONV_W))
            a, bx = _rg_gates(y, wg, gb, sp, d)
            a_ref[pl.ds(base, RG_ROWS), :] = a
            bx_ref[pl.ds(base, RG_ROWS), :] = bx
            return carry

        lax.fori_loop(0, seq // RG_ROWS, gate_chunk, 0)

        def slab(i):
            r = i if d == 0 else n_rows - 1 - i
            return pl.ds(pl.multiple_of(r * w, w), w)

        def column_totals(i, carry):
            a_tot, h_end = carry
            a = a_ref[slab(i), :]
            return a * a_tot, a * h_end + bx_ref[slab(i), :]

        a_tot, h_end = lax.fori_loop(0, n_rows, column_totals,
                                     (jnp.ones((w, c), F32), jnp.zeros((w, c), F32)))
        a_cum, h_cum = _scan_rows(a_tot, h_end, reverse=(d == 1))
        after = a_cum * h0[d] + h_cum
        w_idx = lax.broadcasted_iota(jnp.int32, (w, 1), 0)
        if d == 0:
            h_in = jnp.where(w_idx == 0, h0[d], pltpu.roll(after, 1, 0))
        else:
            h_in = jnp.where(w_idx == w - 1, h0[d], pltpu.roll(after, w - 1, 0))

        def emit(i, h):
            h = a_ref[slab(i), :] * h + bx_ref[slab(i), :]
            if d == 0:
                hf_ref[slab(i), :] = h
            else:
                ol_ref[slab(i), :] = ((hf_ref[slab(i), :] + h) * _gelu_tanh(gl_ref[slab(i), :])).astype(BF16)
            return h

        lax.fori_loop(0, n_rows, emit, h_in)


def _rglru(p, conv_w, conv_b, wg, gb, lam, n_batch, seq, ctx_len, col0):
    cw_total = conv_w.shape[1]
    n_slabs = cw_total // RG_SLAB
    n_rows = seq // GRID_W
    ctx_base = n_batch * seq // ctx_len
    lat = lambda colblk: pl.BlockSpec((seq, RG_SLAB), lambda b, j: (b, colblk * n_slabs + j))
    ctx = lambda colblk: pl.BlockSpec((ctx_len, RG_SLAB), lambda b, j: (ctx_base + b, colblk * n_slabs + j))
    return pl.pallas_call(
        functools.partial(_rglru_kernel, n_rows=n_rows),
        grid=(n_batch, n_slabs),
        in_specs=[
            lat(col0), ctx(col0), lat(col0 + 1), ctx(col0 + 1),
            pl.BlockSpec((CONV_W, RG_SLAB), lambda b, j: (0, j)),
            pl.BlockSpec((1, RG_SLAB), lambda b, j: (0, j)),
            pl.BlockSpec((None, RG_SLAB, 4 * RG_SLAB), lambda b, j: (j, 0, 0)),
            pl.BlockSpec((None, 1, 4 * RG_SLAB), lambda b, j: (j, 0, 0)),
            pl.BlockSpec((2, RG_SLAB), lambda b, j: (0, j)),
        ],
        out_specs=[pl.BlockSpec((seq, RG_SLAB), lambda b, j: (b, j)),
                   pl.BlockSpec((ctx_len, RG_SLAB), lambda b, j: (b, j))],
        out_shape=[jax.ShapeDtypeStruct((n_batch * seq, cw_total), BF16),
                   jax.ShapeDtypeStruct((n_batch * ctx_len, cw_total), BF16)],
        scratch_shapes=[pltpu.VMEM((seq + 3 * GRID_W, RG_SLAB), F32),
                        pltpu.VMEM((seq, RG_SLAB), F32), pltpu.VMEM((seq, RG_SLAB), F32),
                        pltpu.VMEM((seq, RG_SLAB), F32)],
        compiler_params=_cparams(("parallel", "arbitrary")),
        name="rglru",
    )(p, p, p, p, conv_w, conv_b, wg, gb, lam)


def _merge_kernel(ya_ref, yb_ref, yc_ref, ga_ref, gb_ref, gc_ref, h_ref, mod_ref, bw_ref, wo_ref,
                  o_ref, *, mod_base):
    j = pl.program_id(1)

    @pl.when(j == 0)
    def _():
        o_ref[...] = jnp.zeros_like(o_ref)

    merged = (_sigmoid(ga_ref[...]) * _dot(ya_ref[...], bw_ref[0])
              + _sigmoid(gb_ref[...]) * _dot(yb_ref[...], bw_ref[1])
              + _sigmoid(gc_ref[...]) * _dot(yc_ref[...], bw_ref[2]))
    o_ref[...] += _dot(merged.astype(BF16), wo_ref[...])

    @pl.when(j == pl.num_programs(1) - 1)
    def _():
        o_ref[...] = h_ref[...] + mod_ref[mod_base:mod_base + 1, :] * o_ref[...]


def _merge(h, n_rows, p, ya, yb, yc, modtab, mod_row, branch_w, w_out, layer, mod_base):
    d = h.shape[1]
    bw = ya.shape[1]
    n_j = d // MERGE_TILE
    gate0 = 11 * bw // MERGE_TILE
    per_branch = d // MERGE_TILE
    branch = pl.BlockSpec((MERGE_ROWS, bw), lambda i, j: (i, 0))
    gate = lambda k: pl.BlockSpec((MERGE_ROWS, MERGE_TILE), lambda i, j: (i, gate0 + k * per_branch + j))
    return pl.pallas_call(
        functools.partial(_merge_kernel, mod_base=mod_base),
        grid=(n_rows // MERGE_ROWS, n_j),
        in_specs=[branch, branch, branch, gate(0), gate(1), gate(2),
                  pl.BlockSpec((MERGE_ROWS, d), lambda i, j: (i, 0)),
                  pl.BlockSpec((None, N_MOD, d), lambda i, j: (mod_row(MERGE_ROWS)(i), 0, 0)),
                  pl.BlockSpec((None, 3, bw, MERGE_TILE), lambda i, j: (layer, 0, 0, j)),
                  pl.BlockSpec((None, MERGE_TILE, d), lambda i, j: (layer, j, 0))],
        out_specs=pl.BlockSpec((MERGE_ROWS, d), lambda i, j: (i, 0)),
        out_shape=jax.ShapeDtypeStruct((n_rows, d), F32),
        compiler_params=_cparams(("parallel", "arbitrary")),
        name="merge",
    )(ya, yb, yc, p, p, p, h, modtab, branch_w, w_out)


def kernel(x, c, ctx, c_ctx, mod_w, mod_b, norm_g, ffn_w_in, ffn_w_out, w_in, mlstm_gate_b,
           hgrn_lb_logits, conv_w, conv_b, rg_gate_w, rg_gate_b, rg_lambda, branch_w, w_out, final_g):
    n_batch, seq, d = x.shape
    ctx_len = ctx.shape[1]
    depth = mod_w.shape[0]
    aw = branch_w.shape[2]
    dh = aw // A_HEADS
    n_lat_rows = n_batch * seq
    n_rows = n_lat_rows + n_batch * ctx_len
    assert seq % ROW_TILE == 0 and (n_batch * ctx_len) % ROW_TILE == 0
    assert seq % LA == 0 and ctx_len % LA == 0 and seq % LB == 0 and ctx_len % LB == 0
    assert seq % GRID_W == 0 and seq % RG_ROWS == 0 and n_lat_rows % ctx_len == 0
    assert n_batch + 1 <= MOD_ROWS and aw % RG_SLAB == 0 and RG_SLAB % (aw // C_BLOCKS) == 0

    def mod_row(tile):
        return lambda i: jnp.where(i < n_lat_rows // tile, 1 + i // (seq // tile), 0)

    c16 = jnp.zeros((MOD_ROWS, d), F32).at[0].set(c_ctx).at[1:1 + n_batch].set(c)
    modtab = _mod_table(c16, mod_w, mod_b).reshape(depth, MOD_ROWS, N_MOD, d)
    ffn_w_in_b = ffn_w_in.astype(BF16)
    ffn_w_out_b = ffn_w_out.astype(BF16)
    g0 = 4 * aw
    n_gate = 4 * A_HEADS
    w_main = jnp.concatenate([w_in[:, :, :g0], w_in[:, :, g0 + n_gate:]], axis=-1).astype(BF16)
    w_gate = jnp.pad(w_in[:, :, g0:g0 + n_gate], ((0, 0), (0, 0), (0, GATE_PAD - n_gate))).astype(BF16)
    gate_bias = jnp.pad(mlstm_gate_b.reshape(depth, 1, n_gate), ((0, 0), (0, 0), (0, GATE_PAD - n_gate)))
    lb_p = jax.nn.softmax(hgrn_lb_logits.astype(F32), axis=0)
    lb_all = jnp.cumsum(lb_p, axis=0) - lb_p[0:1]
    branch_w_b = branch_w.astype(BF16)
    w_out_b = w_out.astype(BF16)
    n_slabs = aw // RG_SLAB
    c_db = aw // C_BLOCKS
    per_slab = RG_SLAB // c_db
    blocks = rg_gate_w.reshape(depth, 2, 2, n_slabs, per_slab, c_db, c_db)
    eye = jnp.eye(per_slab, dtype=F32)
    dense = jnp.einsum('ldgspio,pq->ldgspiqo', blocks, eye).reshape(depth, 2, 2, n_slabs, RG_SLAB, RG_SLAB)
    rg_w = dense.transpose(0, 3, 4, 1, 2, 5).reshape(depth, n_slabs, RG_SLAB, 4 * RG_SLAB).astype(BF16)
    rg_b = (rg_gate_b.reshape(depth, 2, 2, n_slabs, RG_SLAB).transpose(0, 3, 1, 2, 4)
            .reshape(depth, n_slabs, 1, 4 * RG_SLAB))
    tri = np.tril(np.ones((LA, LA), np.float32))
    tri3 = [jnp.asarray(np.concatenate([m] * 3, axis=1), BF16) for m in (tri, tri.T)]
    hgrn_consts = []
    for direction in (0, 1):
        m3, masks = _hgrn_constants(LB, direction)
        hgrn_consts.append((jnp.asarray(m3, BF16), jnp.asarray(masks, F32)))

    h = jnp.concatenate([x.reshape(n_lat_rows, d), ctx.reshape(n_batch * ctx_len, d)], axis=0)
    for layer in range(depth):
        last = layer == depth - 1
        mt = modtab[layer]
        ng = norm_g[layer]
        h = _ffn(h, n_rows, mt, mod_row, ng[0:1], ffn_w_in_b, ffn_w_out_b, layer, 0, 0)
        p, gates = _inproj(h, mt, mod_row, ng[1:2], w_main, w_gate, layer, 3)
        a_fwd = _mlstm(p, gates, gate_bias[layer], tri3[0], n_batch, seq, ctx_len, dh, 0)
        ya = _mlstm(p, gates, gate_bias[layer], tri3[1], n_batch, seq, ctx_len, dh, 1, o_fwd=a_fwd)
        b_fwd = _hgrn(p, lb_all[layer, 0:1], hgrn_consts[0], n_batch, seq, ctx_len, 4, 0)
        yb = _hgrn(p, lb_all[layer, 1:2], hgrn_consts[1], n_batch, seq, ctx_len, 4, 1, o_fwd=b_fwd)
        yc_lat, yc_ctx = _rglru(p, conv_w[layer], conv_b[layer][None, :], rg_w[layer], rg_b[layer],
                                rg_lambda[layer], n_batch, seq, ctx_len, 9)
        yc = jnp.concatenate([yc_lat, yc_ctx], axis=0)
        rows_out = n_lat_rows if last else n_rows
        h = _merge(h, rows_out, p, ya, yb, yc, mt, mod_row, branch_w_b, w_out_b, layer, 5)
        h = _ffn(h, rows_out, mt, mod_row, ng[2:3], ffn_w_in_b, ffn_w_out_b, layer, 1, 6,
                 final_g=final_g[None, :] if last else None)
    return h.reshape(n_batch, seq, d)
```

```python
import functools

import numpy as np
import jax
import jax.numpy as jnp
from jax import lax
from jax.experimental import pallas as pl
from jax.experimental.pallas import tpu as pltpu

EPS = 1e-6
N_MOD = 9
GRID_W = 64
A_HEADS = 4
B_DK = 128
C_BLOCKS = 16
CONV_W = 4
RG_C = 8.0
NEG_BIG = -1e30

MOD_ROWS = 16
GATE_PAD = 128
ROW_TILE = 1024
FF_TILE = 512
FFN_SUB = 512
PROJ_TILE = 1024
MOD_TILE = 1024
MERGE_TILE = 512
MERGE_ROWS = 512
LA = 256
LB = 128
A_PARTS = 3
B_PARTS = 2
RG_SLAB = 256
RG_ROWS = 512
VMEM_LIMIT = 60 * 1024 * 1024

F32 = jnp.float32
BF16 = jnp.bfloat16


def _cparams(sem):
    return pltpu.CompilerParams(dimension_semantics=sem, vmem_limit_bytes=VMEM_LIMIT)


def _dot(a, b):
    return jnp.dot(a, b, preferred_element_type=F32)


def _dot_nt(a, b):
    return lax.dot_general(a, b, (((1,), (1,)), ((), ())), preferred_element_type=F32)


def _dot01(m_rep, x):
    parts = m_rep.shape[1] // x.shape[0]
    pieces = []
    rest = x
    for _ in range(parts):
        piece = rest.astype(BF16)
        pieces.append(piece)
        rest = rest - piece.astype(F32)
    return _dot(m_rep, jnp.concatenate(pieces, axis=0))


def _sigmoid(x):
    return jax.nn.sigmoid(x)


def _softplus(z):
    return jnp.maximum(z, 0.0) + jnp.log1p(jnp.exp(-jnp.abs(z)))


def _gelu_tanh(x):
    return 0.5 * x * (1.0 + jnp.tanh(np.sqrt(2.0 / np.pi).astype(np.float32) * (x + 0.044715 * (x * x * x))))


def _rms(x):
    return x * lax.rsqrt(jnp.mean(x * x, axis=-1, keepdims=True) + EPS)


def _adaln(h, g, shift, scale):
    return (_rms(h) * g) * (1.0 + scale) + shift


def _mod_kernel(c_ref, w_ref, b_ref, o_ref):
    cv = c_ref[...]
    sc = (cv * _sigmoid(cv)).astype(BF16)
    o_ref[...] = _dot(sc, w_ref[...].astype(BF16)) + b_ref[...]


def _mod_table(c16, mod_w, mod_b):
    depth, d, n = mod_w.shape
    return pl.pallas_call(
        _mod_kernel,
        grid=(depth, n // MOD_TILE),
        in_specs=[
            pl.BlockSpec((MOD_ROWS, d), lambda l, j: (0, 0)),
            pl.BlockSpec((None, d, MOD_TILE), lambda l, j: (l, 0, j)),
            pl.BlockSpec((None, 1, MOD_TILE), lambda l, j: (l, 0, j)),
        ],
        out_specs=pl.BlockSpec((None, MOD_ROWS, MOD_TILE), lambda l, j: (l, 0, j)),
        out_shape=jax.ShapeDtypeStruct((depth, MOD_ROWS, n), F32),
        compiler_params=_cparams(("arbitrary", "arbitrary")),
        name="mod_table",
    )(c16, mod_w, mod_b.reshape(depth, 1, n))


def _ffn_kernel(h_ref, mod_ref, g_ref, wg_ref, wu_ref, wo_ref, *rest, mod_base, final):
    if final:
        fg_ref, o_ref, xn_ref = rest
    else:
        o_ref, xn_ref = rest
    j = pl.program_id(1)
    subs = [slice(r, r + FFN_SUB) for r in range(0, h_ref.shape[0], FFN_SUB)]

    @pl.when(j == 0)
    def _():
        for rows in subs:
            xn = _adaln(h_ref[rows, :], g_ref[...], mod_ref[mod_base:mod_base + 1, :],
                        mod_ref[mod_base + 1:mod_base + 2, :])
            xn_ref[rows, :] = xn.astype(BF16)
            o_ref[rows, :] = jnp.zeros((FFN_SUB, o_ref.shape[1]), F32)

    for rows in subs:
        xn = xn_ref[rows, :]
        gate = _dot(xn, wg_ref[...])
        up = _dot(xn, wu_ref[...])
        act = (gate * _sigmoid(gate)) * up
        o_ref[rows, :] += _dot(act.astype(BF16), wo_ref[...])

    @pl.when(j == pl.num_programs(1) - 1)
    def _():
        for rows in subs:
            out = h_ref[rows, :] + (0.5 * mod_ref[mod_base + 2:mod_base + 3, :]) * o_ref[rows, :]
            if final:
                out = _rms(out) * fg_ref[...]
            o_ref[rows, :] = out


def _ffn(h, n_rows, modtab, mod_row, g, w_in, w_out, layer, which, mod_base, final_g=None):
    d = h.shape[1]
    d_ff = w_out.shape[2]
    n_ff = d_ff // FF_TILE
    final = final_g is not None
    in_specs = [
        pl.BlockSpec((ROW_TILE, d), lambda i, j: (i, 0)),
        pl.BlockSpec((None, N_MOD, d), lambda i, j: (mod_row(ROW_TILE)(i), 0, 0)),
        pl.BlockSpec((1, d), lambda i, j: (0, 0)),
        pl.BlockSpec((None, None, d, FF_TILE), lambda i, j: (layer, which, 0, j)),
        pl.BlockSpec((None, None, d, FF_TILE), lambda i, j: (layer, which, 0, j + n_ff)),
        pl.BlockSpec((None, None, FF_TILE, d), lambda i, j: (layer, which, j, 0)),
    ]
    args = [h, modtab, g, w_in, w_in, w_out]
    if final:
        in_specs.append(pl.BlockSpec((1, d), lambda i, j: (0, 0)))
        args.append(final_g)
    return pl.pallas_call(
        functools.partial(_ffn_kernel, mod_base=mod_base, final=final),
        grid=(n_rows // ROW_TILE, n_ff),
        in_specs=in_specs,
        out_specs=pl.BlockSpec((ROW_TILE, d), lambda i, j: (i, 0)),
        out_shape=jax.ShapeDtypeStruct((n_rows, d), F32),
        scratch_shapes=[pltpu.VMEM((ROW_TILE, d), BF16)],
        compiler_params=_cparams(("parallel", "arbitrary")),
        name="ffn",
    )(*args)


def _inproj_kernel(h_ref, mod_ref, g_ref, w_ref, wgate_ref, p16_ref, pf_ref, gate_ref, xn_ref,
                   *, mod_base, n16):
    j = pl.program_id(1)

    @pl.when(j == 0)
    def _():
        xn = _adaln(h_ref[...], g_ref[...], mod_ref[mod_base:mod_base + 1, :],
                    mod_ref[mod_base + 1:mod_base + 2, :]).astype(BF16)
        xn_ref[...] = xn
        gate_ref[...] = _dot(xn, wgate_ref[...])

    res = _dot(xn_ref[...], w_ref[...])

    @pl.when(j < n16)
    def _():
        p16_ref[...] = res.astype(BF16)

    @pl.when(j >= n16)
    def _():
        pf_ref[...] = res


def _inproj(h, modtab, mod_row, g, w_main, w_gate, layer, mod_base, n16):
    t, d = h.shape
    n = w_main.shape[2]
    n_tiles = n // PROJ_TILE
    return pl.pallas_call(
        functools.partial(_inproj_kernel, mod_base=mod_base, n16=n16),
        grid=(t // ROW_TILE, n_tiles),
        in_specs=[
            pl.BlockSpec((ROW_TILE, d), lambda i, j: (i, 0)),
            pl.BlockSpec((None, N_MOD, d), lambda i, j: (mod_row(ROW_TILE)(i), 0, 0)),
            pl.BlockSpec((1, d), lambda i, j: (0, 0)),
            pl.BlockSpec((None, d, PROJ_TILE), lambda i, j: (layer, 0, j)),
            pl.BlockSpec((None, d, GATE_PAD), lambda i, j: (layer, 0, 0)),
        ],
        out_specs=[
            pl.BlockSpec((ROW_TILE, PROJ_TILE), lambda i, j: (i, jnp.minimum(j, n16 - 1))),
            pl.BlockSpec((ROW_TILE, PROJ_TILE), lambda i, j: (i, jnp.maximum(j - n16, 0))),
            pl.BlockSpec((ROW_TILE, GATE_PAD), lambda i, j: (i, 0)),
        ],
        out_shape=[jax.ShapeDtypeStruct((t, n16 * PROJ_TILE), BF16),
                   jax.ShapeDtypeStruct((t, (n_tiles - n16) * PROJ_TILE), F32),
                   jax.ShapeDtypeStruct((t, GATE_PAD), F32)],
        scratch_shapes=[pltpu.VMEM((ROW_TILE, d), BF16)],
        compiler_params=_cparams(("parallel", "arbitrary")),
        name="inproj",
    )(h, modtab, g, w_main, w_gate)


def _chunk_block(n_batch, seq, ctx_len, length, direction):
    n_lat = seq // length
    n_ctx = ctx_len // length
    ctx_base = n_batch * seq // length

    def block(b, s):
        if direction == 0:
            return jnp.where(s < n_ctx, ctx_base + b * n_ctx + s, b * n_lat + (s - n_ctx))
        return jnp.where(s < n_ctx, ctx_base + b * n_ctx + (n_ctx - 1 - s),
                         b * n_lat + (n_lat - 1 - (s - n_ctx)))

    return block, n_ctx + n_lat


def _mlstm_kernel(q_ref, k_ref, v_ref, g_ref, *rest, dh, direction, final):
    if final:
        of_ref, og_ref, bias_ref, tri_ref, o_ref, c_ref, n_ref, m_ref = rest
    else:
        bias_ref, tri_ref, o_ref, c_ref, n_ref, m_ref = rest
    length = q_ref.shape[0]

    @pl.when(pl.program_id(1) == 0)
    def _():
        c_ref[...] = jnp.zeros_like(c_ref)
        n_ref[...] = jnp.zeros_like(n_ref)
        m_ref[...] = jnp.zeros_like(m_ref)

    row = lax.broadcasted_iota(jnp.int32, (length, length), 0)
    col = lax.broadcasted_iota(jnp.int32, (length, length), 1)
    mask = (col <= row) if direction == 0 else (col >= row)
    last = length - 1 if direction == 0 else 0
    pre = g_ref[...] + bias_ref[...]
    logf = jnp.minimum(pre, 0.0) - jnp.log1p(jnp.exp(-jnp.abs(pre)))
    b_all = _dot01(tri_ref[...], logf)
    b_all_t = b_all.T
    pre_t = pre.T
    for hh in range(A_HEADS):
        ci = direction * 2 * A_HEADS + hh
        cf = ci + A_HEADS
        sl = slice(hh * dh, (hh + 1) * dh)
        b_col = b_all[:, cf:cf + 1]
        b_row = b_all_t[cf:cf + 1, :]
        i_col = pre[:, ci:ci + 1]
        i_row = pre_t[ci:ci + 1, :]
        m_prev = m_ref[hh][:, 0:1]
        cmat = c_ref[hh]
        nvec = n_ref[hh]

        dmat = jnp.where(mask, b_col - b_row + i_row, NEG_BIG)
        inter = b_col + m_prev
        m_t = jnp.maximum(inter, jnp.max(dmat, axis=-1, keepdims=True))
        w_inter = jnp.exp(inter - m_t)
        q = q_ref[:, sl]
        qf = q.astype(F32)
        kf = k_ref[:, sl].astype(F32) * (dh ** -0.5)
        v = v_ref[:, sl]
        s = _dot_nt(q, kf.astype(BF16)) * jnp.exp(dmat - m_t)
        num = w_inter * _dot(q, cmat.astype(BF16)) + _dot(s.astype(BF16), v)
        den = (w_inter * jnp.sum(qf * nvec, axis=-1, keepdims=True)
               + jnp.sum(s, axis=-1, keepdims=True))
        out = num / jnp.maximum(jnp.abs(den), jnp.exp(-m_t))
        if final:
            o_ref[:, sl] = (_rms(out + of_ref[:, sl]) * _sigmoid(og_ref[:, sl].astype(F32))).astype(BF16)
        else:
            o_ref[:, sl] = out

        b_last = b_all[last:last + 1, cf:cf + 1]
        wlog = b_last - b_col + i_col
        m_new = jnp.maximum(b_last + m_prev, jnp.max(wlog, axis=0, keepdims=True))
        decay = jnp.exp(b_last + m_prev - m_new)
        kw = kf * jnp.exp(wlog - m_new)
        c_ref[hh] = decay * cmat + _dot(kw.T.astype(BF16), v)
        n_ref[hh] = decay * nvec + jnp.sum(kw, axis=0, keepdims=True)
        m_ref[hh] = jnp.broadcast_to(m_new, m_ref.shape[1:])


def _mlstm(p, gates, gate_bias, tri3, n_batch, seq, ctx_len, dh, direction, o_fwd=None):
    t = p.shape[0]
    aw = A_HEADS * dh
    block, steps = _chunk_block(n_batch, seq, ctx_len, LA, direction)
    final = o_fwd is not None
    spec = lambda colblk, width: pl.BlockSpec((LA, width), lambda b, s: (block(b, s), colblk))
    const = lambda shape: pl.BlockSpec(shape, lambda b, s: (0,) * len(shape))
    in_specs = [spec(0, aw), spec(1, aw), spec(2, aw), spec(0, GATE_PAD)]
    args = [p, p, p, gates]
    if final:
        in_specs += [spec(0, aw), spec(3, aw)]
        args += [o_fwd, p]
    in_specs += [const((1, GATE_PAD)), const((LA, A_PARTS * LA))]
    args += [gate_bias, tri3]
    return pl.pallas_call(
        functools.partial(_mlstm_kernel, dh=dh, direction=direction, final=final),
        grid=(n_batch, steps),
        in_specs=in_specs,
        out_specs=spec(0, aw),
        out_shape=jax.ShapeDtypeStruct((t, aw), BF16 if final else F32),
        scratch_shapes=[pltpu.VMEM((A_HEADS, dh, dh), F32),
                        pltpu.VMEM((A_HEADS, 1, dh), F32),
                        pltpu.VMEM((A_HEADS, 1, 128), F32)],
        compiler_params=_cparams(("parallel", "arbitrary")),
        name="mlstm_bwd" if final else "mlstm_fwd",
    )(*args)


def _hgrn_levels(length):
    levels = []
    c = length // 2
    while c >= 1:
        levels.append(c)
        c //= 2
    return levels


def _hgrn_constants(length, direction):
    t = np.arange(length)[:, None]
    u = np.arange(length)[None, :]
    if direction == 0:
        mats = [u <= t, u > t]
    else:
        mats = [u >= t, u < t]
    masks = []
    for c in _hgrn_levels(length):
        blk = t // (2 * c)
        pos = t % (2 * c)
        if direction == 0:
            ref = blk * 2 * c + c - 1
            q_role = pos >= c
            m = np.where(q_role, (u > ref) & (u <= t), (u > t) & (u <= ref))
            keep = (blk == blk.T) & q_role & (pos.T < c)
        else:
            ref = blk * 2 * c + c
            q_role = pos < c
            m = np.where(q_role, (u >= t) & (u < ref), (u >= ref) & (u < t))
            keep = (blk == blk.T) & q_role & (pos.T >= c)
        mats.append(m)
        masks.append(keep)
    m_all = np.concatenate(mats, 0).astype(np.float32)
    return np.concatenate([m_all] * B_PARTS, axis=1), np.stack(masks).astype(np.float32)


def _hgrn_kernel(q_ref, f_ref, v_ref, *rest, direction, final):
    if final:
        of_ref, og_ref, lb_ref, m3_ref, mask_ref, o_ref, st_ref, d_ref = rest
    else:
        lb_ref, m3_ref, mask_ref, o_ref, st_ref, d_ref = rest
    length, width = q_ref.shape
    n_heads = width // B_DK
    levels = _hgrn_levels(length)
    last = length - 1 if direction == 0 else 0

    @pl.when(pl.program_id(1) == 0)
    def _():
        st_ref[...] = jnp.zeros_like(st_ref)

    t_idx = lax.broadcasted_iota(jnp.int32, (length, 1), 0)
    eye = (lax.broadcasted_iota(jnp.int32, (length, length), 0)
           == lax.broadcasted_iota(jnp.int32, (length, length), 1))
    lb = lb_ref[...]
    f = lb + (1.0 - lb) * _sigmoid(f_ref[...])
    kk = 1.0 - f
    qpre = q_ref[...].astype(F32)
    q = qpre * _sigmoid(qpre)
    d_ref[...] = _dot01(m3_ref[...], jnp.log(f))
    q_in = (q * jnp.exp(d_ref[0:length, :])).astype(BF16)
    k_st = kk * jnp.exp(d_ref[length:2 * length, :])
    decay = jnp.exp(d_ref[last:last + 1, :])
    xs = []
    for li, c in enumerate(levels):
        pos = t_idx % (2 * c)
        q_role = (pos >= c) if direction == 0 else (pos < c)
        e = jnp.exp(d_ref[(2 + li) * length:(3 + li) * length, :])
        xs.append((jnp.where(q_role, q, kk) * e).astype(BF16))
    qk = q * kk
    v_all = v_ref[...]
    for hd in range(n_heads):
        sl = slice(hd * B_DK, (hd + 1) * B_DK)
        att = jnp.where(eye, jnp.sum(qk[:, sl], axis=-1, keepdims=True), 0.0)
        for li in range(len(levels)):
            x = xs[li][:, sl]
            att = att + mask_ref[li] * _dot_nt(x, x)
        vh = v_all[:, sl]
        st = st_ref[hd]
        out = _dot(att.astype(BF16), vh) + _dot_nt(q_in[:, sl], st.astype(BF16))
        if final:
            og = og_ref[:, sl].astype(F32)
            o_ref[:, sl] = (_rms(out + of_ref[:, sl]) * (og * _sigmoid(og))).astype(BF16)
        else:
            o_ref[:, sl] = out
        st_ref[hd] = st * decay[:, sl] + _dot(vh.astype(F32).T.astype(BF16), k_st[:, sl].astype(BF16))


def _hgrn(p, pf, lb, consts, n_batch, seq, ctx_len, col0, direction, o_fwd=None):
    t = p.shape[0]
    bw = lb.shape[1]
    m3, masks = consts
    n_rows = m3.shape[0]
    n_lev = masks.shape[0]
    block, steps = _chunk_block(n_batch, seq, ctx_len, LB, direction)
    final = o_fwd is not None
    spec = lambda colblk: pl.BlockSpec((LB, bw), lambda b, s: (block(b, s), colblk))
    const = lambda shape: pl.BlockSpec(shape, lambda b, s: (0,) * len(shape))
    in_specs = [spec(col0), spec(direction), spec(col0 + 1)]
    args = [p, pf, p]
    if final:
        in_specs += [spec(0), spec(col0 + 2)]
        args += [o_fwd, p]
    in_specs += [const((1, bw)), const((n_rows, B_PARTS * LB)), const((n_lev, LB, LB))]
    args += [lb, m3, masks]
    return pl.pallas_call(
        functools.partial(_hgrn_kernel, direction=direction, final=final),
        grid=(n_batch, steps),
        in_specs=in_specs,
        out_specs=spec(0),
        out_shape=jax.ShapeDtypeStruct((t, bw), BF16 if final else F32),
        scratch_shapes=[pltpu.VMEM((bw // B_DK, B_DK, B_DK), F32),
                        pltpu.VMEM((n_rows, bw), F32)],
        compiler_params=_cparams(("parallel", "arbitrary")),
        name="hgrn_bwd" if final else "hgrn_fwd",
    )(*args)


def _scan_rows(a, h, reverse):
    n = a.shape[0]
    idx = lax.broadcasted_iota(jnp.int32, (n, 1), 0)
    k = 1
    while k < n:
        if reverse:
            a_s, h_s, valid = pltpu.roll(a, n - k, 0), pltpu.roll(h, n - k, 0), idx < n - k
        else:
            a_s, h_s, valid = pltpu.roll(a, k, 0), pltpu.roll(h, k, 0), idx >= k
        h = jnp.where(valid, a * h_s + h, h)
        a = jnp.where(valid, a_s * a, a)
        k *= 2
    return a, h


def _shift_rows(x, delta):
    n = x.shape[0]
    idx = lax.broadcasted_iota(jnp.int32, (n, 1), 0)
    if delta == 0:
        return x
    y = pltpu.roll(x, (-delta) % n, 0)
    valid = (idx + delta >= 0) & (idx + delta < n)
    return jnp.where(valid, y, 0.0)


def _rg_gates(y, wg, gb, sp, d):
    c = y.shape[1]
    g = _dot(y.astype(BF16), wg[:, 2 * d * c:(2 * d + 2) * c]) + gb[:, 2 * d * c:(2 * d + 2) * c]
    r = _sigmoid(g[:, :c])
    i = _sigmoid(g[:, c:])
    log_a = (-RG_C) * r * sp[d:d + 1, :]
    a = jnp.exp(log_a)
    return a, jnp.sqrt(1.0 - a * a) * (i * y)


def _rglru_kernel(xl_ref, xc_ref, gl_ref, gc_ref, cw_ref, cb_ref, wg_ref, gb_ref, lam_ref,
                  ol_ref, oc_ref, xe_ref, a_ref, bx_ref, hf_ref, *, n_rows):
    w = GRID_W
    seq, c = xl_ref.shape
    sp = _softplus(-lam_ref[...])
    cw = cw_ref[...]
    cb = cb_ref[...]
    wg = wg_ref[...]
    gb = gb_ref[...]

    xc = xc_ref[...].astype(F32)
    yc = cb + sum(cw[j:j + 1, :] * _shift_rows(xc, j - 2) for j in range(CONV_W))
    a, bx = _rg_gates(yc, wg, gb, sp, 0)
    _, hs_f = _scan_rows(a, bx, reverse=False)
    a, bx = _rg_gates(yc, wg, gb, sp, 1)
    _, hs_b = _scan_rows(a, bx, reverse=True)
    n_ctx = xc.shape[0]
    h0 = (hs_f[n_ctx - 1:n_ctx, :], hs_b[0:1, :])
    oc_ref[...] = ((hs_f + hs_b) * _gelu_tanh(gc_ref[...].astype(F32))).astype(BF16)

    xe_ref[2 * w:2 * w + seq, :] = xl_ref[...].astype(F32)
    xe_ref[0:w, :] = _shift_rows(xl_ref[(n_rows - 2) * w:(n_rows - 1) * w, :].astype(F32), -1)
    xe_ref[w:2 * w, :] = _shift_rows(xl_ref[(n_rows - 1) * w:n_rows * w, :].astype(F32), -1)
    xe_ref[2 * w + seq:3 * w + seq, :] = _shift_rows(xl_ref[0:w, :].astype(F32), 1)

    for d in range(2):
        def gate_chunk(i, carry):
            base = pl.multiple_of(i * RG_ROWS, RG_ROWS)
            y = cb + sum(cw[j:j + 1, :] * xe_ref[pl.ds(base + j * w, RG_ROWS), :] for j in range(CONV_W))
            a, bx = _rg_gates(y, wg, gb, sp, d)
            a_ref[pl.ds(base, RG_ROWS), :] = a
            bx_ref[pl.ds(base, RG_ROWS), :] = bx
            return carry

        lax.fori_loop(0, seq // RG_ROWS, gate_chunk, 0)

        def slab(i):
            r = i if d == 0 else n_rows - 1 - i
            return pl.ds(pl.multiple_of(r * w, w), w)

        def column_totals(i, carry):
            a_tot, h_end = carry
            a = a_ref[slab(i), :]
            return a * a_tot, a * h_end + bx_ref[slab(i), :]

        a_tot, h_end = lax.fori_loop(0, n_rows, column_totals,
                                     (jnp.ones((w, c), F32), jnp.zeros((w, c), F32)))
        a_cum, h_cum = _scan_rows(a_tot, h_end, reverse=(d == 1))
        after = a_cum * h0[d] + h_cum
        w_idx = lax.broadcasted_iota(jnp.int32, (w, 1), 0)
        if d == 0:
            h_in = jnp.where(w_idx == 0, h0[d], pltpu.roll(after, 1, 0))
        else:
            h_in = jnp.where(w_idx == w - 1, h0[d], pltpu.roll(after, w - 1, 0))

        def emit(i, h):
            h = a_ref[slab(i), :] * h + bx_ref[slab(i), :]
            if d == 0:
                hf_ref[slab(i), :] = h
            else:
                gate = _gelu_tanh(gl_ref[slab(i), :].astype(F32))
                ol_ref[slab(i), :] = ((hf_ref[slab(i), :] + h) * gate).astype(BF16)
            return h

        lax.fori_loop(0, n_rows, emit, h_in)


def _rglru(p, conv_w, conv_b, wg, gb, lam, n_batch, seq, ctx_len, col0):
    cw_total = conv_w.shape[1]
    n_slabs = cw_total // RG_SLAB
    n_rows = seq // GRID_W
    ctx_base = n_batch * seq // ctx_len
    lat = lambda colblk: pl.BlockSpec((seq, RG_SLAB), lambda b, j: (b, colblk * n_slabs + j))
    ctx = lambda colblk: pl.BlockSpec((ctx_len, RG_SLAB), lambda b, j: (ctx_base + b, colblk * n_slabs + j))
    return pl.pallas_call(
        functools.partial(_rglru_kernel, n_rows=n_rows),
        grid=(n_batch, n_slabs),
        in_specs=[
            lat(col0), ctx(col0), lat(col0 + 1), ctx(col0 + 1),
            pl.BlockSpec((CONV_W, RG_SLAB), lambda b, j: (0, j)),
            pl.BlockSpec((1, RG_SLAB), lambda b, j: (0, j)),
            pl.BlockSpec((None, RG_SLAB, 4 * RG_SLAB), lambda b, j: (j, 0, 0)),
            pl.BlockSpec((None, 1, 4 * RG_SLAB), lambda b, j: (j, 0, 0)),
            pl.BlockSpec((2, RG_SLAB), lambda b, j: (0, j)),
        ],
        out_specs=[pl.BlockSpec((seq, RG_SLAB), lambda b, j: (b, j)),
                   pl.BlockSpec((ctx_len, RG_SLAB), lambda b, j: (b, j))],
        out_shape=[jax.ShapeDtypeStruct((n_batch * seq, cw_total), BF16),
                   jax.ShapeDtypeStruct((n_batch * ctx_len, cw_total), BF16)],
        scratch_shapes=[pltpu.VMEM((seq + 3 * GRID_W, RG_SLAB), F32),
                        pltpu.VMEM((seq, RG_SLAB), F32), pltpu.VMEM((seq, RG_SLAB), F32),
                        pltpu.VMEM((seq, RG_SLAB), F32)],
        compiler_params=_cparams(("parallel", "arbitrary")),
        name="rglru",
    )(p, p, p, p, conv_w, conv_b, wg, gb, lam)


def _merge_kernel(ya_ref, yb_ref, ycl_ref, ycc_ref, ga_ref, gb_ref, gc_ref, h_ref, mod_ref, bw_ref,
                  wo_ref, o_ref, *, mod_base, n_lat_tiles):
    j = pl.program_id(1)

    @pl.when(j == 0)
    def _():
        o_ref[...] = jnp.zeros_like(o_ref)

    yc = jnp.where(pl.program_id(0) < n_lat_tiles, ycl_ref[...], ycc_ref[...])
    merged = (_sigmoid(ga_ref[...].astype(F32)) * _dot(ya_ref[...], bw_ref[0])
              + _sigmoid(gb_ref[...].astype(F32)) * _dot(yb_ref[...], bw_ref[1])
              + _sigmoid(gc_ref[...].astype(F32)) * _dot(yc, bw_ref[2]))
    o_ref[...] += _dot(merged.astype(BF16), wo_ref[...])

    @pl.when(j == pl.num_programs(1) - 1)
    def _():
        o_ref[...] = h_ref[...] + mod_ref[mod_base:mod_base + 1, :] * o_ref[...]


def _merge(h, n_rows, p, ya, yb, yc_lat, yc_ctx, modtab, mod_row, branch_w, w_out, layer, mod_base):
    d = h.shape[1]
    bw = ya.shape[1]
    n_j = d // MERGE_TILE
    gate0 = 9 * bw // MERGE_TILE
    per_branch = d // MERGE_TILE
    n_lat_tiles = yc_lat.shape[0] // MERGE_ROWS
    branch = pl.BlockSpec((MERGE_ROWS, bw), lambda i, j: (i, 0))
    branch_lat = pl.BlockSpec((MERGE_ROWS, bw), lambda i, j: (jnp.minimum(i, n_lat_tiles - 1), 0))
    branch_ctx = pl.BlockSpec((MERGE_ROWS, bw), lambda i, j: (jnp.maximum(i - n_lat_tiles, 0), 0))
    gate = lambda k: pl.BlockSpec((MERGE_ROWS, MERGE_TILE), lambda i, j: (i, gate0 + k * per_branch + j))
    return pl.pallas_call(
        functools.partial(_merge_kernel, mod_base=mod_base, n_lat_tiles=n_lat_tiles),
        grid=(n_rows // MERGE_ROWS, n_j),
        in_specs=[branch, branch, branch_lat, branch_ctx, gate(0), gate(1), gate(2),
                  pl.BlockSpec((MERGE_ROWS, d), lambda i, j: (i, 0)),
                  pl.BlockSpec((None, N_MOD, d), lambda i, j: (mod_row(MERGE_ROWS)(i), 0, 0)),
                  pl.BlockSpec((None, 3, bw, MERGE_TILE), lambda i, j: (layer, 0, 0, j)),
                  pl.BlockSpec((None, MERGE_TILE, d), lambda i, j: (layer, j, 0))],
        out_specs=pl.BlockSpec((MERGE_ROWS, d), lambda i, j: (i, 0)),
        out_shape=jax.ShapeDtypeStruct((n_rows, d), F32),
        compiler_params=_cparams(("parallel", "arbitrary")),
        name="merge",
    )(ya, yb, yc_lat, yc_ctx, p, p, p, h, modtab, branch_w, w_out)


def kernel(x, c, ctx, c_ctx, mod_w, mod_b, norm_g, ffn_w_in, ffn_w_out, w_in, mlstm_gate_b,
           hgrn_lb_logits, conv_w, conv_b, rg_gate_w, rg_gate_b, rg_lambda, branch_w, w_out, final_g):
    n_batch, seq, d = x.shape
    ctx_len = ctx.shape[1]
    depth = mod_w.shape[0]
    aw = branch_w.shape[2]
    dh = aw // A_HEADS
    n_lat_rows = n_batch * seq
    n_rows = n_lat_rows + n_batch * ctx_len
    assert seq % ROW_TILE == 0 and (n_batch * ctx_len) % ROW_TILE == 0
    assert seq % LA == 0 and ctx_len % LA == 0 and seq % LB == 0 and ctx_len % LB == 0
    assert seq % GRID_W == 0 and seq % RG_ROWS == 0 and n_lat_rows % ctx_len == 0
    assert n_batch + 1 <= MOD_ROWS and aw % RG_SLAB == 0 and RG_SLAB % (aw // C_BLOCKS) == 0

    def mod_row(tile):
        return lambda i: jnp.where(i < n_lat_rows // tile, 1 + i // (seq // tile), 0)

    c16 = jnp.zeros((MOD_ROWS, d), F32).at[0].set(c_ctx).at[1:1 + n_batch].set(c)
    modtab = _mod_table(c16, mod_w, mod_b).reshape(depth, MOD_ROWS, N_MOD, d)
    ffn_w_in_b = ffn_w_in.astype(BF16)
    ffn_w_out_b = ffn_w_out.astype(BF16)
    g0 = 4 * aw
    n_gate = 4 * A_HEADS
    b0 = g0 + n_gate
    w_main = jnp.concatenate([w_in[:, :, :g0], w_in[:, :, b0:b0 + aw], w_in[:, :, b0 + 3 * aw:],
                              w_in[:, :, b0 + aw:b0 + 3 * aw]], axis=-1).astype(BF16)
    n16 = (w_main.shape[2] - 2 * aw) // PROJ_TILE
    w_gate = jnp.pad(w_in[:, :, g0:g0 + n_gate], ((0, 0), (0, 0), (0, GATE_PAD - n_gate))).astype(BF16)
    gate_bias = jnp.pad(mlstm_gate_b.reshape(depth, 1, n_gate), ((0, 0), (0, 0), (0, GATE_PAD - n_gate)))
    lb_p = jax.nn.softmax(hgrn_lb_logits.astype(F32), axis=0)
    lb_all = jnp.cumsum(lb_p, axis=0) - lb_p[0:1]
    branch_w_b = branch_w.astype(BF16)
    w_out_b = w_out.astype(BF16)
    n_slabs = aw // RG_SLAB
    c_db = aw // C_BLOCKS
    per_slab = RG_SLAB // c_db
    blocks = rg_gate_w.reshape(depth, 2, 2, n_slabs, per_slab, c_db, c_db)
    eye = jnp.eye(per_slab, dtype=F32)
    dense = jnp.einsum('ldgspio,pq->ldgspiqo', blocks, eye).reshape(depth, 2, 2, n_slabs, RG_SLAB, RG_SLAB)
    rg_w = dense.transpose(0, 3, 4, 1, 2, 5).reshape(depth, n_slabs, RG_SLAB, 4 * RG_SLAB).astype(BF16)
    rg_b = (rg_gate_b.reshape(depth, 2, 2, n_slabs, RG_SLAB).transpose(0, 3, 1, 2, 4)
            .reshape(depth, n_slabs, 1, 4 * RG_SLAB))
    tri = np.tril(np.ones((LA, LA), np.float32))
    tri3 = [jnp.asarray(np.concatenate([m] * A_PARTS, axis=1), BF16) for m in (tri, tri.T)]
    hgrn_consts = []
    for direction in (0, 1):
        m3, masks = _hgrn_constants(LB, direction)
        hgrn_consts.append((jnp.asarray(m3, BF16), jnp.asarray(masks, F32)))

    h = jnp.concatenate([x.reshape(n_lat_rows, d), ctx.reshape(n_batch * ctx_len, d)], axis=0)
    for layer in range(depth):
        last = layer == depth - 1
        mt = modtab[layer]
        ng = norm_g[layer]
        h = _ffn(h, n_rows, mt, mod_row, ng[0:1], ffn_w_in_b, ffn_w_out_b, layer, 0, 0)
        p, pf, gates = _inproj(h, mt, mod_row, ng[1:2], w_main, w_gate, layer, 3, n16)
        a_fwd = _mlstm(p, gates, gate_bias[layer], tri3[0], n_batch, seq, ctx_len, dh, 0)
        ya = _mlstm(p, gates, gate_bias[layer], tri3[1], n_batch, seq, ctx_len, dh, 1, o_fwd=a_fwd)
        b_fwd = _hgrn(p, pf, lb_all[layer, 0:1], hgrn_consts[0], n_batch, seq, ctx_len, 4, 0)
        yb = _hgrn(p, pf, lb_all[layer, 1:2], hgrn_consts[1], n_batch, seq, ctx_len, 4, 1, o_fwd=b_fwd)
        yc_lat, yc_ctx = _rglru(p, conv_w[layer], conv_b[layer][None, :], rg_w[layer], rg_b[layer],
                                rg_lambda[layer], n_batch, seq, ctx_len, 7)
        rows_out = n_lat_rows if last else n_rows
        h = _merge(h, rows_out, p, ya, yb, yc_lat, yc_ctx, mt, mod_row, branch_w_b, w_out_b, layer, 5)
        h = _ffn(h, rows_out, mt, mod_row, ng[2:3], ffn_w_in_b, ffn_w_out_b, layer, 1, 6,
                 final_g=final_g[None, :] if last else None)
    return h.reshape(n_batch, seq, d)
```

```python
import functools

import numpy as np
import jax
import jax.numpy as jnp
from jax import lax
from jax.experimental import pallas as pl
from jax.experimental.pallas import tpu as pltpu

EPS = 1e-6
N_MOD = 9
GRID_W = 64
A_HEADS = 4
B_DK = 128
C_BLOCKS = 16
CONV_W = 4
RG_C = 8.0
NEG_BIG = -1e30

MOD_ROWS = 16
GATE_PAD = 128
ROW_TILE = 1024
FF_TILE = 512
FFN_SUB = 512
PROJ_TILE = 1024
MOD_TILE = 1024
MERGE_TILE = 512
MERGE_ROWS = 512
LA = 256
LB = 128
A_PARTS = 3
B_PARTS = 2
RG_SLAB = 256
RG_ROWS = 512
VMEM_LIMIT = 60 * 1024 * 1024

F32 = jnp.float32
BF16 = jnp.bfloat16


def _cparams(sem):
    return pltpu.CompilerParams(dimension_semantics=sem, vmem_limit_bytes=VMEM_LIMIT)


def _dot(a, b):
    return jnp.dot(a, b, preferred_element_type=F32)


def _dot_nt(a, b):
    return lax.dot_general(a, b, (((1,), (1,)), ((), ())), preferred_element_type=F32)


def _dot01(m_rep, x):
    parts = m_rep.shape[1] // x.shape[0]
    pieces = []
    rest = x
    for _ in range(parts):
        piece = rest.astype(BF16)
        pieces.append(piece)
        rest = rest - piece.astype(F32)
    return _dot(m_rep, jnp.concatenate(pieces, axis=0))


def _sigmoid(x):
    return jax.nn.sigmoid(x)


def _softplus(z):
    return jnp.maximum(z, 0.0) + jnp.log1p(jnp.exp(-jnp.abs(z)))


def _gelu_tanh(x):
    return 0.5 * x * (1.0 + jnp.tanh(np.sqrt(2.0 / np.pi).astype(np.float32) * (x + 0.044715 * (x * x * x))))


def _rms(x):
    return x * lax.rsqrt(jnp.mean(x * x, axis=-1, keepdims=True) + EPS)


def _adaln(h, g, shift, scale):
    return (_rms(h) * g) * (1.0 + scale) + shift


def _mod_kernel(c_ref, w_ref, b_ref, o_ref):
    cv = c_ref[...]
    sc = (cv * _sigmoid(cv)).astype(BF16)
    o_ref[...] = _dot(sc, w_ref[...].astype(BF16)) + b_ref[...]


def _mod_table(c16, mod_w, mod_b):
    depth, d, n = mod_w.shape
    return pl.pallas_call(
        _mod_kernel,
        grid=(depth, n // MOD_TILE),
        in_specs=[
            pl.BlockSpec((MOD_ROWS, d), lambda l, j: (0, 0)),
            pl.BlockSpec((None, d, MOD_TILE), lambda l, j: (l, 0, j)),
            pl.BlockSpec((None, 1, MOD_TILE), lambda l, j: (l, 0, j)),
        ],
        out_specs=pl.BlockSpec((None, MOD_ROWS, MOD_TILE), lambda l, j: (l, 0, j)),
        out_shape=jax.ShapeDtypeStruct((depth, MOD_ROWS, n), F32),
        compiler_params=_cparams(("arbitrary", "arbitrary")),
        name="mod_table",
    )(c16, mod_w, mod_b.reshape(depth, 1, n))


def _ffn_kernel(h_ref, mod_ref, g_ref, wg_ref, wu_ref, wo_ref, *rest, mod_base, final):
    if final:
        fg_ref, o_ref, xn_ref = rest
    else:
        o_ref, xn_ref = rest
    j = pl.program_id(1)
    subs = [slice(r, r + FFN_SUB) for r in range(0, h_ref.shape[0], FFN_SUB)]

    @pl.when(j == 0)
    def _():
        for rows in subs:
            xn = _adaln(h_ref[rows, :], g_ref[...], mod_ref[mod_base:mod_base + 1, :],
                        mod_ref[mod_base + 1:mod_base + 2, :])
            xn_ref[rows, :] = xn.astype(BF16)
            o_ref[rows, :] = jnp.zeros((FFN_SUB, o_ref.shape[1]), F32)

    for rows in subs:
        xn = xn_ref[rows, :]
        gate = _dot(xn, wg_ref[...])
        up = _dot(xn, wu_ref[...])
        act = (gate * _sigmoid(gate)) * up
        o_ref[rows, :] += _dot(act.astype(BF16), wo_ref[...])

    @pl.when(j == pl.num_programs(1) - 1)
    def _():
        for rows in subs:
            out = h_ref[rows, :] + (0.5 * mod_ref[mod_base + 2:mod_base + 3, :]) * o_ref[rows, :]
            if final:
                out = _rms(out) * fg_ref[...]
            o_ref[rows, :] = out


def _ffn(h, n_rows, modtab, mod_row, g, w_in, w_out, layer, which, mod_base, final_g=None):
    d = h.shape[1]
    d_ff = w_out.shape[2]
    n_ff = d_ff // FF_TILE
    final = final_g is not None
    in_specs = [
        pl.BlockSpec((ROW_TILE, d), lambda i, j: (i, 0)),
        pl.BlockSpec((None, N_MOD, d), lambda i, j: (mod_row(ROW_TILE)(i), 0, 0)),
        pl.BlockSpec((1, d), lambda i, j: (0, 0)),
        pl.BlockSpec((None, None, d, FF_TILE), lambda i, j: (layer, which, 0, j)),
        pl.BlockSpec((None, None, d, FF_TILE), lambda i, j: (layer, which, 0, j + n_ff)),
        pl.BlockSpec((None, None, FF_TILE, d), lambda i, j: (layer, which, j, 0)),
    ]
    args = [h, modtab, g, w_in, w_in, w_out]
    if final:
        in_specs.append(pl.BlockSpec((1, d), lambda i, j: (0, 0)))
        args.append(final_g)
    return pl.pallas_call(
        functools.partial(_ffn_kernel, mod_base=mod_base, final=final),
        grid=(n_rows // ROW_TILE, n_ff),
        in_specs=in_specs,
        out_specs=pl.BlockSpec((ROW_TILE, d), lambda i, j: (i, 0)),
        out_shape=jax.ShapeDtypeStruct((n_rows, d), F32),
        scratch_shapes=[pltpu.VMEM((ROW_TILE, d), BF16)],
        compiler_params=_cparams(("parallel", "arbitrary")),
        name="ffn",
    )(*args)


def _inproj_kernel(h_ref, mod_ref, g_ref, w_ref, wgate_ref, p16_ref, pf_ref, gate_ref, xn_ref,
                   *, mod_base, n16):
    j = pl.program_id(1)

    @pl.when(j == 0)
    def _():
        xn = _adaln(h_ref[...], g_ref[...], mod_ref[mod_base:mod_base + 1, :],
                    mod_ref[mod_base + 1:mod_base + 2, :]).astype(BF16)
        xn_ref[...] = xn
        gate_ref[...] = _dot(xn, wgate_ref[...])

    @pl.when(j < n16)
    def _():
        p16_ref[...] = _dot(xn_ref[...], w_ref[...]).astype(BF16)

    @pl.when(j >= n16)
    def _():
        pf_ref[...] = _dot(xn_ref[...], w_ref[...])


def _inproj(h, modtab, mod_row, g, w_main, w_gate, layer, mod_base, n16):
    t, d = h.shape
    n = w_main.shape[2]
    n_tiles = n // PROJ_TILE
    return pl.pallas_call(
        functools.partial(_inproj_kernel, mod_base=mod_base, n16=n16),
        grid=(t // ROW_TILE, n_tiles),
        in_specs=[
            pl.BlockSpec((ROW_TILE, d), lambda i, j: (i, 0)),
            pl.BlockSpec((None, N_MOD, d), lambda i, j: (mod_row(ROW_TILE)(i), 0, 0)),
            pl.BlockSpec((1, d), lambda i, j: (0, 0)),
            pl.BlockSpec((None, d, PROJ_TILE), lambda i, j: (layer, 0, j)),
            pl.BlockSpec((None, d, GATE_PAD), lambda i, j: (layer, 0, 0)),
        ],
        out_specs=[
            pl.BlockSpec((ROW_TILE, PROJ_TILE), lambda i, j: (i, jnp.minimum(j, n16 - 1))),
            pl.BlockSpec((ROW_TILE, PROJ_TILE), lambda i, j: (i, jnp.maximum(j - n16, 0))),
            pl.BlockSpec((ROW_TILE, GATE_PAD), lambda i, j: (i, 0)),
        ],
        out_shape=[jax.ShapeDtypeStruct((t, n16 * PROJ_TILE), BF16),
                   jax.ShapeDtypeStruct((t, (n_tiles - n16) * PROJ_TILE), F32),
                   jax.ShapeDtypeStruct((t, GATE_PAD), F32)],
        scratch_shapes=[pltpu.VMEM((ROW_TILE, d), BF16)],
        compiler_params=_cparams(("parallel", "arbitrary")),
        name="inproj",
    )(h, modtab, g, w_main, w_gate)


def _chunk_block(n_batch, seq, ctx_len, length, direction):
    n_lat = seq // length
    n_ctx = ctx_len // length
    ctx_base = n_batch * seq // length

    def block(b, s):
        if direction == 0:
            return jnp.where(s < n_ctx, ctx_base + b * n_ctx + s, b * n_lat + (s - n_ctx))
        return jnp.where(s < n_ctx, ctx_base + b * n_ctx + (n_ctx - 1 - s),
                         b * n_lat + (n_lat - 1 - (s - n_ctx)))

    return block, n_ctx + n_lat


def _mlstm_kernel(q_ref, k_ref, v_ref, g_ref, *rest, dh, direction, final):
    if final:
        of_ref, og_ref, bias_ref, tri_ref, o_ref, c_ref, n_ref, m_ref = rest
    else:
        bias_ref, tri_ref, o_ref, c_ref, n_ref, m_ref = rest
    length = q_ref.shape[0]

    @pl.when(pl.program_id(1) == 0)
    def _():
        c_ref[...] = jnp.zeros_like(c_ref)
        n_ref[...] = jnp.zeros_like(n_ref)
        m_ref[...] = jnp.zeros_like(m_ref)

    row = lax.broadcasted_iota(jnp.int32, (length, length), 0)
    col = lax.broadcasted_iota(jnp.int32, (length, length), 1)
    mask = (row <= col) if direction == 0 else (row >= col)
    last = length - 1 if direction == 0 else 0
    pre = g_ref[...] + bias_ref[...]
    logf = jnp.minimum(pre, 0.0) - jnp.log1p(jnp.exp(-jnp.abs(pre)))
    b_all = _dot01(tri_ref[...], logf)
    b_all_t = b_all.T
    pre_t = pre.T
    for hh in range(A_HEADS):
        ci = direction * 2 * A_HEADS + hh
        cf = ci + A_HEADS
        sl = slice(hh * dh, (hh + 1) * dh)
        b_row = b_all_t[cf:cf + 1, :]
        r_row = pre_t[ci:ci + 1, :] - b_row
        r_col = pre[:, ci:ci + 1] - b_all[:, cf:cf + 1]
        m_prev = m_ref[hh][:, 0:1]
        c_t = c_ref[hh]
        nvec = n_ref[hh]

        dmat = jnp.where(mask, r_col + b_row, NEG_BIG)
        inter = b_row + m_prev
        m_t = jnp.maximum(inter, jnp.max(dmat, axis=0, keepdims=True))
        w_inter = jnp.exp(inter - m_t)
        q = q_ref[:, sl]
        k = (k_ref[:, sl].astype(F32) * (dh ** -0.5)).astype(BF16)
        v_t = v_ref[:, sl].astype(F32).T
        s_t = _dot_nt(k, q) * jnp.exp(dmat - m_t)
        num_t = w_inter * _dot_nt(c_t.astype(BF16), q) + _dot(v_t.astype(BF16), s_t.astype(BF16))
        n_rows = jnp.broadcast_to(nvec, (16, dh)).astype(BF16)
        den = w_inter * _dot_nt(n_rows, q)[0:1, :] + jnp.sum(s_t, axis=0, keepdims=True)
        out = (num_t / jnp.maximum(jnp.abs(den), jnp.exp(-m_t))).T
        if final:
            o_ref[:, sl] = (_rms(out + of_ref[:, sl]) * _sigmoid(og_ref[:, sl].astype(F32))).astype(BF16)
        else:
            o_ref[:, sl] = out

        b_last = b_all_t[cf:cf + 1, last:last + 1]
        wlog = b_last + r_row
        m_new = jnp.maximum(b_last + m_prev, jnp.max(wlog, axis=-1, keepdims=True))
        decay = jnp.exp(b_last + m_prev - m_new)
        ws = jnp.exp(wlog - m_new)
        c_ref[hh] = decay * c_t + _dot((v_t * ws).astype(BF16), k)
        ws_rows = jnp.broadcast_to(ws, (16, length)).astype(BF16)
        n_ref[hh] = decay * nvec + _dot(ws_rows, k)[0:1, :]
        m_ref[hh] = jnp.broadcast_to(m_new, m_ref.shape[1:])


def _mlstm(p, gates, gate_bias, tri3, n_batch, seq, ctx_len, dh, direction, o_fwd=None):
    t = p.shape[0]
    aw = A_HEADS * dh
    block, steps = _chunk_block(n_batch, seq, ctx_len, LA, direction)
    final = o_fwd is not None
    spec = lambda colblk, width: pl.BlockSpec((LA, width), lambda b, s: (block(b, s), colblk))
    const = lambda shape: pl.BlockSpec(shape, lambda b, s: (0,) * len(shape))
    in_specs = [spec(0, aw), spec(1, aw), spec(2, aw), spec(0, GATE_PAD)]
    args = [p, p, p, gates]
    if final:
        in_specs += [spec(0, aw), spec(3, aw)]
        args += [o_fwd, p]
    in_specs += [const((1, GATE_PAD)), const((LA, A_PARTS * LA))]
    args += [gate_bias, tri3]
    return pl.pallas_call(
        functools.partial(_mlstm_kernel, dh=dh, direction=direction, final=final),
        grid=(n_batch, steps),
        in_specs=in_specs,
        out_specs=spec(0, aw),
        out_shape=jax.ShapeDtypeStruct((t, aw), BF16 if final else F32),
        scratch_shapes=[pltpu.VMEM((A_HEADS, dh, dh), F32),
                        pltpu.VMEM((A_HEADS, 1, dh), F32),
                        pltpu.VMEM((A_HEADS, 1, 128), F32)],
        compiler_params=_cparams(("parallel", "arbitrary")),
        name="mlstm_bwd" if final else "mlstm_fwd",
    )(*args)


def _hgrn_levels(length):
    levels = []
    c = length // 2
    while c >= 1:
        levels.append(c)
        c //= 2
    return levels


def _hgrn_constants(length, direction):
    t = np.arange(length)[:, None]
    u = np.arange(length)[None, :]
    if direction == 0:
        mats = [u <= t, u > t]
    else:
        mats = [u >= t, u < t]
    masks = []
    for c in _hgrn_levels(length):
        blk = t // (2 * c)
        pos = t % (2 * c)
        if direction == 0:
            ref = blk * 2 * c + c - 1
            q_role = pos >= c
            m = np.where(q_role, (u > ref) & (u <= t), (u > t) & (u <= ref))
            keep = (blk == blk.T) & q_role & (pos.T < c)
        else:
            ref = blk * 2 * c + c
            q_role = pos < c
            m = np.where(q_role, (u >= t) & (u < ref), (u >= ref) & (u < t))
            keep = (blk == blk.T) & q_role & (pos.T >= c)
        mats.append(m)
        masks.append(keep)
    m_all = np.concatenate(mats, 0).astype(np.float32)
    return np.concatenate([m_all] * B_PARTS, axis=1), np.stack(masks).astype(np.float32)


def _hgrn_kernel(q_ref, f_ref, v_ref, *rest, direction, final):
    if final:
        of_ref, og_ref, lb_ref, m3_ref, mask_ref, o_ref, st_ref, d_ref = rest
    else:
        lb_ref, m3_ref, mask_ref, o_ref, st_ref, d_ref = rest
    length, width = q_ref.shape
    n_heads = width // B_DK
    levels = _hgrn_levels(length)
    last = length - 1 if direction == 0 else 0

    @pl.when(pl.program_id(1) == 0)
    def _():
        st_ref[...] = jnp.zeros_like(st_ref)

    t_idx = lax.broadcasted_iota(jnp.int32, (length, 1), 0)
    eye = (lax.broadcasted_iota(jnp.int32, (length, length), 0)
           == lax.broadcasted_iota(jnp.int32, (length, length), 1))
    lb = lb_ref[...]
    f = lb + (1.0 - lb) * _sigmoid(f_ref[...])
    kk = 1.0 - f
    qpre = q_ref[...].astype(F32)
    q = qpre * _sigmoid(qpre)
    d_ref[...] = _dot01(m3_ref[...], jnp.log(f))
    q_in = (q * jnp.exp(d_ref[0:length, :])).astype(BF16)
    k_st = kk * jnp.exp(d_ref[length:2 * length, :])
    decay = jnp.exp(d_ref[last:last + 1, :])
    xs = []
    for li, c in enumerate(levels):
        pos = t_idx % (2 * c)
        q_role = (pos >= c) if direction == 0 else (pos < c)
        e = jnp.exp(d_ref[(2 + li) * length:(3 + li) * length, :])
        xs.append((jnp.where(q_role, q, kk) * e).astype(BF16))
    qk = q * kk
    v_all = v_ref[...]
    for hd in range(n_heads):
        sl = slice(hd * B_DK, (hd + 1) * B_DK)
        att = jnp.where(eye, jnp.sum(qk[:, sl], axis=-1, keepdims=True), 0.0)
        for li in range(len(levels)):
            x = xs[li][:, sl]
            att = att + mask_ref[li] * _dot_nt(x, x)
        vh = v_all[:, sl]
        st = st_ref[hd]
        out = _dot(att.astype(BF16), vh) + _dot_nt(q_in[:, sl], st.astype(BF16))
        if final:
            og = og_ref[:, sl].astype(F32)
            o_ref[:, sl] = (_rms(out + of_ref[:, sl]) * (og * _sigmoid(og))).astype(BF16)
        else:
            o_ref[:, sl] = out
        st_ref[hd] = st * decay[:, sl] + _dot(vh.astype(F32).T.astype(BF16), k_st[:, sl].astype(BF16))


def _hgrn(p, pf, lb, consts, n_batch, seq, ctx_len, col0, direction, o_fwd=None):
    t = p.shape[0]
    bw = lb.shape[1]
    m3, masks = consts
    n_rows = m3.shape[0]
    n_lev = masks.shape[0]
    block, steps = _chunk_block(n_batch, seq, ctx_len, LB, direction)
    final = o_fwd is not None
    spec = lambda colblk: pl.BlockSpec((LB, bw), lambda b, s: (block(b, s), colblk))
    const = lambda shape: pl.BlockSpec(shape, lambda b, s: (0,) * len(shape))
    in_specs = [spec(col0), spec(direction), spec(col0 + 1)]
    args = [p, pf, p]
    if final:
        in_specs += [spec(0), spec(col0 + 2)]
        args += [o_fwd, p]
    in_specs += [const((1, bw)), const((n_rows, B_PARTS * LB)), const((n_lev, LB, LB))]
    args += [lb, m3, masks]
    return pl.pallas_call(
        functools.partial(_hgrn_kernel, direction=direction, final=final),
        grid=(n_batch, steps),
        in_specs=in_specs,
        out_specs=spec(0),
        out_shape=jax.ShapeDtypeStruct((t, bw), BF16 if final else F32),
        scratch_shapes=[pltpu.VMEM((bw // B_DK, B_DK, B_DK), F32),
                        pltpu.VMEM((n_rows, bw), F32)],
        compiler_params=_cparams(("parallel", "arbitrary")),
        name="hgrn_bwd" if final else "hgrn_fwd",
    )(*args)


def _scan_rows(a, h, reverse):
    n = a.shape[0]
    idx = lax.broadcasted_iota(jnp.int32, (n, 1), 0)
    k = 1
    while k < n:
        if reverse:
            a_s, h_s, valid = pltpu.roll(a, n - k, 0), pltpu.roll(h, n - k, 0), idx < n - k
        else:
            a_s, h_s, valid = pltpu.roll(a, k, 0), pltpu.roll(h, k, 0), idx >= k
        h = jnp.where(valid, a * h_s + h, h)
        a = jnp.where(valid, a_s * a, a)
        k *= 2
    return a, h


def _shift_rows(x, delta):
    n = x.shape[0]
    idx = lax.broadcasted_iota(jnp.int32, (n, 1), 0)
    if delta == 0:
        return x
    y = pltpu.roll(x, (-delta) % n, 0)
    valid = (idx + delta >= 0) & (idx + delta < n)
    return jnp.where(valid, y, 0.0)


def _rg_gates(y, wg, gb, sp, d):
    c = y.shape[1]
    g = _dot(y.astype(BF16), wg[:, 2 * d * c:(2 * d + 2) * c]) + gb[:, 2 * d * c:(2 * d + 2) * c]
    r = _sigmoid(g[:, :c])
    i = _sigmoid(g[:, c:])
    log_a = (-RG_C) * r * sp[d:d + 1, :]
    a = jnp.exp(log_a)
    return a, jnp.sqrt(1.0 - a * a) * (i * y)


def _rglru_kernel(xl_ref, xc_ref, gl_ref, gc_ref, cw_ref, cb_ref, wg_ref, gb_ref, lam_ref,
                  ol_ref, oc_ref, xe_ref, a_ref, bx_ref, hf_ref, *, n_rows):
    w = GRID_W
    seq, c = xl_ref.shape
    sp = _softplus(-lam_ref[...])
    cw = cw_ref[...]
    cb = cb_ref[...]
    wg = wg_ref[...]
    gb = gb_ref[...]

    xc = xc_ref[...].astype(F32)
    yc = cb + sum(cw[j:j + 1, :] * _shift_rows(xc, j - 2) for j in range(CONV_W))
    a, bx = _rg_gates(yc, wg, gb, sp, 0)
    _, hs_f = _scan_rows(a, bx, reverse=False)
    a, bx = _rg_gates(yc, wg, gb, sp, 1)
    _, hs_b = _scan_rows(a, bx, reverse=True)
    n_ctx = xc.shape[0]
    h0 = (hs_f[n_ctx - 1:n_ctx, :], hs_b[0:1, :])
    oc_ref[...] = ((hs_f + hs_b) * _gelu_tanh(gc_ref[...].astype(F32))).astype(BF16)

    xe_ref[2 * w:2 * w + seq, :] = xl_ref[...].astype(F32)
    xe_ref[0:w, :] = _shift_rows(xl_ref[(n_rows - 2) * w:(n_rows - 1) * w, :].astype(F32), -1)
    xe_ref[w:2 * w, :] = _shift_rows(xl_ref[(n_rows - 1) * w:n_rows * w, :].astype(F32), -1)
    xe_ref[2 * w + seq:3 * w + seq, :] = _shift_rows(xl_ref[0:w, :].astype(F32), 1)

    for d in range(2):
        def gate_chunk(i, carry):
            base = pl.multiple_of(i * RG_ROWS, RG_ROWS)
            y = cb + sum(cw[j:j + 1, :] * xe_ref[pl.ds(base + j * w, RG_ROWS), :] for j in range(CONV_W))
            a, bx = _rg_gates(y, wg, gb, sp, d)
            a_ref[pl.ds(base, RG_ROWS), :] = a
            bx_ref[pl.ds(base, RG_ROWS), :] = bx
            return carry

        lax.fori_loop(0, seq // RG_ROWS, gate_chunk, 0)

        def slab(i):
            r = i if d == 0 else n_rows - 1 - i
            return pl.ds(pl.multiple_of(r * w, w), w)

        def column_totals(i, carry):
            a_tot, h_end = carry
            a = a_ref[slab(i), :]
            return a * a_tot, a * h_end + bx_ref[slab(i), :]

        a_tot, h_end = lax.fori_loop(0, n_rows, column_totals,
                                     (jnp.ones((w, c), F32), jnp.zeros((w, c), F32)))
        a_cum, h_cum = _scan_rows(a_tot, h_end, reverse=(d == 1))
        after = a_cum * h0[d] + h_cum
        w_idx = lax.broadcasted_iota(jnp.int32, (w, 1), 0)
        if d == 0:
            h_in = jnp.where(w_idx == 0, h0[d], pltpu.roll(after, 1, 0))
        else:
            h_in = jnp.where(w_idx == w - 1, h0[d], pltpu.roll(after, w - 1, 0))

        def emit(i, h):
            h = a_ref[slab(i), :] * h + bx_ref[slab(i), :]
            if d == 0:
                hf_ref[slab(i), :] = h
            else:
                gate = _gelu_tanh(gl_ref[slab(i), :].astype(F32))
                ol_ref[slab(i), :] = ((hf_ref[slab(i), :] + h) * gate).astype(BF16)
            return h

        lax.fori_loop(0, n_rows, emit, h_in)


def _rglru(p, conv_w, conv_b, wg, gb, lam, n_batch, seq, ctx_len, col0):
    cw_total = conv_w.shape[1]
    n_slabs = cw_total // RG_SLAB
    n_rows = seq // GRID_W
    ctx_base = n_batch * seq // ctx_len
    lat = lambda colblk: pl.BlockSpec((seq, RG_SLAB), lambda b, j: (b, colblk * n_slabs + j))
    ctx = lambda colblk: pl.BlockSpec((ctx_len, RG_SLAB), lambda b, j: (ctx_base + b, colblk * n_slabs + j))
    return pl.pallas_call(
        functools.partial(_rglru_kernel, n_rows=n_rows),
        grid=(n_batch, n_slabs),
        in_specs=[
            lat(col0), ctx(col0), lat(col0 + 1), ctx(col0 + 1),
            pl.BlockSpec((CONV_W, RG_SLAB), lambda b, j: (0, j)),
            pl.BlockSpec((1, RG_SLAB), lambda b, j: (0, j)),
            pl.BlockSpec((None, RG_SLAB, 4 * RG_SLAB), lambda b, j: (j, 0, 0)),
            pl.BlockSpec((None, 1, 4 * RG_SLAB), lambda b, j: (j, 0, 0)),
            pl.BlockSpec((2, RG_SLAB), lambda b, j: (0, j)),
        ],
        out_specs=[pl.BlockSpec((seq, RG_SLAB), lambda b, j: (b, j)),
                   pl.BlockSpec((ctx_len, RG_SLAB), lambda b, j: (b, j))],
        out_shape=[jax.ShapeDtypeStruct((n_batch * seq, cw_total), BF16),
                   jax.ShapeDtypeStruct((n_batch * ctx_len, cw_total), BF16)],
        scratch_shapes=[pltpu.VMEM((seq + 3 * GRID_W, RG_SLAB), F32),
                        pltpu.VMEM((seq, RG_SLAB), F32), pltpu.VMEM((seq, RG_SLAB), F32),
                        pltpu.VMEM((seq, RG_SLAB), F32)],
        compiler_params=_cparams(("parallel", "arbitrary")),
        name="rglru",
    )(p, p, p, p, conv_w, conv_b, wg, gb, lam)


def _merge_kernel(ya_ref, yb_ref, ycl_ref, ycc_ref, ga_ref, gb_ref, gc_ref, h_ref, mod_ref, bw_ref,
                  wo_ref, o_ref, *, mod_base, n_lat_tiles):
    j = pl.program_id(1)

    @pl.when(j == 0)
    def _():
        o_ref[...] = jnp.zeros_like(o_ref)

    yc = jnp.where(pl.program_id(0) < n_lat_tiles, ycl_ref[...], ycc_ref[...])
    merged = (_sigmoid(ga_ref[...].astype(F32)) * _dot(ya_ref[...], bw_ref[0])
              + _sigmoid(gb_ref[...].astype(F32)) * _dot(yb_ref[...], bw_ref[1])
              + _sigmoid(gc_ref[...].astype(F32)) * _dot(yc, bw_ref[2]))
    o_ref[...] += _dot(merged.astype(BF16), wo_ref[...])

    @pl.when(j == pl.num_programs(1) - 1)
    def _():
        o_ref[...] = h_ref[...] + mod_ref[mod_base:mod_base + 1, :] * o_ref[...]


def _merge(h, n_rows, p, ya, yb, yc_lat, yc_ctx, modtab, mod_row, branch_w, w_out, layer, mod_base):
    d = h.shape[1]
    bw = ya.shape[1]
    n_j = d // MERGE_TILE
    gate0 = 9 * bw // MERGE_TILE
    per_branch = d // MERGE_TILE
    n_lat_tiles = yc_lat.shape[0] // MERGE_ROWS
    branch = pl.BlockSpec((MERGE_ROWS, bw), lambda i, j: (i, 0))
    branch_lat = pl.BlockSpec((MERGE_ROWS, bw), lambda i, j: (jnp.minimum(i, n_lat_tiles - 1), 0))
    branch_ctx = pl.BlockSpec((MERGE_ROWS, bw), lambda i, j: (jnp.maximum(i - n_lat_tiles, 0), 0))
    gate = lambda k: pl.BlockSpec((MERGE_ROWS, MERGE_TILE), lambda i, j: (i, gate0 + k * per_branch + j))
    return pl.pallas_call(
        functools.partial(_merge_kernel, mod_base=mod_base, n_lat_tiles=n_lat_tiles),
        grid=(n_rows // MERGE_ROWS, n_j),
        in_specs=[branch, branch, branch_lat, branch_ctx, gate(0), gate(1), gate(2),
                  pl.BlockSpec((MERGE_ROWS, d), lambda i, j: (i, 0)),
                  pl.BlockSpec((None, N_MOD, d), lambda i, j: (mod_row(MERGE_ROWS)(i), 0, 0)),
                  pl.BlockSpec((None, 3, bw, MERGE_TILE), lambda i, j: (layer, 0, 0, j)),
                  pl.BlockSpec((None, MERGE_TILE, d), lambda i, j: (layer, j, 0))],
        out_specs=pl.BlockSpec((MERGE_ROWS, d), lambda i, j: (i, 0)),
        out_shape=jax.ShapeDtypeStruct((n_rows, d), F32),
        compiler_params=_cparams(("parallel", "arbitrary")),
        name="merge",
    )(ya, yb, yc_lat, yc_ctx, p, p, p, h, modtab, branch_w, w_out)


def kernel(x, c, ctx, c_ctx, mod_w, mod_b, norm_g, ffn_w_in, ffn_w_out, w_in, mlstm_gate_b,
           hgrn_lb_logits, conv_w, conv_b, rg_gate_w, rg_gate_b, rg_lambda, branch_w, w_out, final_g):
    n_batch, seq, d = x.shape
    ctx_len = ctx.shape[1]
    depth = mod_w.shape[0]
    aw = branch_w.shape[2]
    dh = aw // A_HEADS
    n_lat_rows = n_batch * seq
    n_rows = n_lat_rows + n_batch * ctx_len
    assert seq % ROW_TILE == 0 and (n_batch * ctx_len) % ROW_TILE == 0
    assert seq % LA == 0 and ctx_len % LA == 0 and seq % LB == 0 and ctx_len % LB == 0
    assert seq % GRID_W == 0 and seq % RG_ROWS == 0 and n_lat_rows % ctx_len == 0
    assert n_batch + 1 <= MOD_ROWS and aw % RG_SLAB == 0 and RG_SLAB % (aw // C_BLOCKS) == 0

    def mod_row(tile):
        return lambda i: jnp.where(i < n_lat_rows // tile, 1 + i // (seq // tile), 0)

    c16 = jnp.zeros((MOD_ROWS, d), F32).at[0].set(c_ctx).at[1:1 + n_batch].set(c)
    modtab = _mod_table(c16, mod_w, mod_b).reshape(depth, MOD_ROWS, N_MOD, d)
    ffn_w_in_b = ffn_w_in.astype(BF16)
    ffn_w_out_b = ffn_w_out.astype(BF16)
    g0 = 4 * aw
    n_gate = 4 * A_HEADS
    b0 = g0 + n_gate
    w_main = jnp.concatenate([w_in[:, :, :g0], w_in[:, :, b0:b0 + aw], w_in[:, :, b0 + 3 * aw:],
                              w_in[:, :, b0 + aw:b0 + 3 * aw]], axis=-1).astype(BF16)
    n16 = (w_main.shape[2] - 2 * aw) // PROJ_TILE
    w_gate = jnp.pad(w_in[:, :, g0:g0 + n_gate], ((0, 0), (0, 0), (0, GATE_PAD - n_gate))).astype(BF16)
    gate_bias = jnp.pad(mlstm_gate_b.reshape(depth, 1, n_gate), ((0, 0), (0, 0), (0, GATE_PAD - n_gate)))
    lb_p = jax.nn.softmax(hgrn_lb_logits.astype(F32), axis=0)
    lb_all = jnp.cumsum(lb_p, axis=0) - lb_p[0:1]
    branch_w_b = branch_w.astype(BF16)
    w_out_b = w_out.astype(BF16)
    n_slabs = aw // RG_SLAB
    c_db = aw // C_BLOCKS
    per_slab = RG_SLAB // c_db
    blocks = rg_gate_w.reshape(depth, 2, 2, n_slabs, per_slab, c_db, c_db)
    eye = jnp.eye(per_slab, dtype=F32)
    dense = jnp.einsum('ldgspio,pq->ldgspiqo', blocks, eye).reshape(depth, 2, 2, n_slabs, RG_SLAB, RG_SLAB)
    rg_w = dense.transpose(0, 3, 4, 1, 2, 5).reshape(depth, n_slabs, RG_SLAB, 4 * RG_SLAB).astype(BF16)
    rg_b = (rg_gate_b.reshape(depth, 2, 2, n_slabs, RG_SLAB).transpose(0, 3, 1, 2, 4)
            .reshape(depth, n_slabs, 1, 4 * RG_SLAB))
    tri = np.tril(np.ones((LA, LA), np.float32))
    tri3 = [jnp.asarray(np.concatenate([m] * A_PARTS, axis=1), BF16) for m in (tri, tri.T)]
    hgrn_consts = []
    for direction in (0, 1):
        m3, masks = _hgrn_constants(LB, direction)
        hgrn_consts.append((jnp.asarray(m3, BF16), jnp.asarray(masks, F32)))

    h = jnp.concatenate([x.reshape(n_lat_rows, d), ctx.reshape(n_batch * ctx_len, d)], axis=0)
    for layer in range(depth):
        last = layer == depth - 1
        mt = modtab[layer]
        ng = norm_g[layer]
        h = _ffn(h, n_rows, mt, mod_row, ng[0:1], ffn_w_in_b, ffn_w_out_b, layer, 0, 0)
        p, pf, gates = _inproj(h, mt, mod_row, ng[1:2], w_main, w_gate, layer, 3, n16)
        a_fwd = _mlstm(p, gates, gate_bias[layer], tri3[0], n_batch, seq, ctx_len, dh, 0)
        ya = _mlstm(p, gates, gate_bias[layer], tri3[1], n_batch, seq, ctx_len, dh, 1, o_fwd=a_fwd)
        b_fwd = _hgrn(p, pf, lb_all[layer, 0:1], hgrn_consts[0], n_batch, seq, ctx_len, 4, 0)
        yb = _hgrn(p, pf, lb_all[layer, 1:2], hgrn_consts[1], n_batch, seq, ctx_len, 4, 1, o_fwd=b_fwd)
        yc_lat, yc_ctx = _rglru(p, conv_w[layer], conv_b[layer][None, :], rg_w[layer], rg_b[layer],
                                rg_lambda[layer], n_batch, seq, ctx_len, 7)
        rows_out = n_lat_rows if last else n_rows
        h = _merge(h, rows_out, p, ya, yb, yc_lat, yc_ctx, mt, mod_row, branch_w_b, w_out_b, layer, 5)
        h = _ffn(h, rows_out, mt, mod_row, ng[2:3], ffn_w_in_b, ffn_w_out_b, layer, 1, 6,
                 final_g=final_g[None, :] if last else None)
    return h.reshape(n_batch, seq, d)
```

```python
import functools

import numpy as np
import jax
import jax.numpy as jnp
from jax import lax
from jax.experimental import pallas as pl
from jax.experimental.pallas import tpu as pltpu

EPS = 1e-6
N_MOD = 9
GRID_W = 64
A_HEADS = 4
B_DK = 128
C_BLOCKS = 16
CONV_W = 4
RG_C = 8.0
NEG_BIG = -1e30
LOG2E = 1.4426950408889634

MOD_ROWS = 16
GATE_PAD = 128
ROW_TILE = 1024
FF_TILE = 512
FFN_SUB = 512
PROJ_TILE = 1024
MOD_TILE = 1024
MERGE_TILE = 512
MERGE_ROWS = 512
LA = 256
LB = 128
A_PARTS = 3
B_PARTS = 2
RG_SLAB = 256
RG_ROWS = 512
VMEM_LIMIT = 60 * 1024 * 1024

F32 = jnp.float32
BF16 = jnp.bfloat16


def _cparams(sem):
    return pltpu.CompilerParams(dimension_semantics=sem, vmem_limit_bytes=VMEM_LIMIT)


def _dot(a, b):
    return jnp.dot(a, b, preferred_element_type=F32)


def _dot_nt(a, b):
    return lax.dot_general(a, b, (((1,), (1,)), ((), ())), preferred_element_type=F32)


def _dot01(m_rep, x):
    parts = m_rep.shape[1] // x.shape[0]
    pieces = []
    rest = x
    for _ in range(parts):
        piece = rest.astype(BF16)
        pieces.append(piece)
        rest = rest - piece.astype(F32)
    return _dot(m_rep, jnp.concatenate(pieces, axis=0))


def _sigmoid(x):
    return jax.nn.sigmoid(x)


def _softplus(z):
    return jnp.maximum(z, 0.0) + jnp.log1p(jnp.exp(-jnp.abs(z)))


def _gelu_tanh(x):
    return 0.5 * x * (1.0 + jnp.tanh(np.sqrt(2.0 / np.pi).astype(np.float32) * (x + 0.044715 * (x * x * x))))


def _rms(x):
    return x * lax.rsqrt(jnp.mean(x * x, axis=-1, keepdims=True) + EPS)


def _adaln(h, g, shift, scale):
    return (_rms(h) * g) * (1.0 + scale) + shift


def _mod_kernel(c_ref, w_ref, b_ref, o_ref):
    cv = c_ref[...]
    sc = (cv * _sigmoid(cv)).astype(BF16)
    o_ref[...] = _dot(sc, w_ref[...].astype(BF16)) + b_ref[...]


def _mod_table(c16, mod_w, mod_b):
    depth, d, n = mod_w.shape
    return pl.pallas_call(
        _mod_kernel,
        grid=(depth, n // MOD_TILE),
        in_specs=[
            pl.BlockSpec((MOD_ROWS, d), lambda l, j: (0, 0)),
            pl.BlockSpec((None, d, MOD_TILE), lambda l, j: (l, 0, j)),
            pl.BlockSpec((None, 1, MOD_TILE), lambda l, j: (l, 0, j)),
        ],
        out_specs=pl.BlockSpec((None, MOD_ROWS, MOD_TILE), lambda l, j: (l, 0, j)),
        out_shape=jax.ShapeDtypeStruct((depth, MOD_ROWS, n), F32),
        compiler_params=_cparams(("arbitrary", "arbitrary")),
        name="mod_table",
    )(c16, mod_w, mod_b.reshape(depth, 1, n))


def _ffn_kernel(h_ref, mod_ref, g_ref, wg_ref, wu_ref, wo_ref, *rest, mod_base, final):
    if final:
        fg_ref, o_ref, xn_ref = rest
    else:
        o_ref, xn_ref = rest
    j = pl.program_id(1)
    subs = [slice(r, r + FFN_SUB) for r in range(0, h_ref.shape[0], FFN_SUB)]

    @pl.when(j == 0)
    def _():
        for rows in subs:
            xn = _adaln(h_ref[rows, :], g_ref[...], mod_ref[mod_base:mod_base + 1, :],
                        mod_ref[mod_base + 1:mod_base + 2, :])
            xn_ref[rows, :] = xn.astype(BF16)
            o_ref[rows, :] = jnp.zeros((FFN_SUB, o_ref.shape[1]), F32)

    for rows in subs:
        xn = xn_ref[rows, :]
        gate = _dot(xn, wg_ref[...])
        up = _dot(xn, wu_ref[...])
        act = (gate * _sigmoid(gate)) * up
        o_ref[rows, :] += _dot(act.astype(BF16), wo_ref[...])

    @pl.when(j == pl.num_programs(1) - 1)
    def _():
        for rows in subs:
            out = h_ref[rows, :] + (0.5 * mod_ref[mod_base + 2:mod_base + 3, :]) * o_ref[rows, :]
            if final:
                out = _rms(out) * fg_ref[...]
            o_ref[rows, :] = out


def _ffn(h, n_rows, modtab, mod_row, g, w_in, w_out, layer, which, mod_base, final_g=None):
    d = h.shape[1]
    d_ff = w_out.shape[2]
    n_ff = d_ff // FF_TILE
    final = final_g is not None
    in_specs = [
        pl.BlockSpec((ROW_TILE, d), lambda i, j: (i, 0)),
        pl.BlockSpec((None, N_MOD, d), lambda i, j: (mod_row(ROW_TILE)(i), 0, 0)),
        pl.BlockSpec((1, d), lambda i, j: (0, 0)),
        pl.BlockSpec((None, None, d, FF_TILE), lambda i, j: (layer, which, 0, j)),
        pl.BlockSpec((None, None, d, FF_TILE), lambda i, j: (layer, which, 0, j + n_ff)),
        pl.BlockSpec((None, None, FF_TILE, d), lambda i, j: (layer, which, j, 0)),
    ]
    args = [h, modtab, g, w_in, w_in, w_out]
    if final:
        in_specs.append(pl.BlockSpec((1, d), lambda i, j: (0, 0)))
        args.append(final_g)
    return pl.pallas_call(
        functools.partial(_ffn_kernel, mod_base=mod_base, final=final),
        grid=(n_rows // ROW_TILE, n_ff),
        in_specs=in_specs,
        out_specs=pl.BlockSpec((ROW_TILE, d), lambda i, j: (i, 0)),
        out_shape=jax.ShapeDtypeStruct((n_rows, d), F32),
        scratch_shapes=[pltpu.VMEM((ROW_TILE, d), BF16)],
        compiler_params=_cparams(("parallel", "arbitrary")),
        name="ffn",
    )(*args)


def _inproj_kernel(h_ref, mod_ref, g_ref, w_ref, wgate_ref, p16_ref, pf_ref, gate_ref, xn_ref,
                   *, mod_base, n16):
    j = pl.program_id(1)

    @pl.when(j == 0)
    def _():
        xn = _adaln(h_ref[...], g_ref[...], mod_ref[mod_base:mod_base + 1, :],
                    mod_ref[mod_base + 1:mod_base + 2, :]).astype(BF16)
        xn_ref[...] = xn
        gate_ref[...] = _dot(xn, wgate_ref[...])

    @pl.when(j < n16)
    def _():
        p16_ref[...] = _dot(xn_ref[...], w_ref[...]).astype(BF16)

    @pl.when(j >= n16)
    def _():
        pf_ref[...] = _dot(xn_ref[...], w_ref[...])


def _inproj(h, modtab, mod_row, g, w_main, w_gate, layer, mod_base, n16):
    t, d = h.shape
    n = w_main.shape[2]
    n_tiles = n // PROJ_TILE
    return pl.pallas_call(
        functools.partial(_inproj_kernel, mod_base=mod_base, n16=n16),
        grid=(t // ROW_TILE, n_tiles),
        in_specs=[
            pl.BlockSpec((ROW_TILE, d), lambda i, j: (i, 0)),
            pl.BlockSpec((None, N_MOD, d), lambda i, j: (mod_row(ROW_TILE)(i), 0, 0)),
            pl.BlockSpec((1, d), lambda i, j: (0, 0)),
            pl.BlockSpec((None, d, PROJ_TILE), lambda i, j: (layer, 0, j)),
            pl.BlockSpec((None, d, GATE_PAD), lambda i, j: (layer, 0, 0)),
        ],
        out_specs=[
            pl.BlockSpec((ROW_TILE, PROJ_TILE), lambda i, j: (i, jnp.minimum(j, n16 - 1))),
            pl.BlockSpec((ROW_TILE, PROJ_TILE), lambda i, j: (i, jnp.maximum(j - n16, 0))),
            pl.BlockSpec((ROW_TILE, GATE_PAD), lambda i, j: (i, 0)),
        ],
        out_shape=[jax.ShapeDtypeStruct((t, n16 * PROJ_TILE), BF16),
                   jax.ShapeDtypeStruct((t, (n_tiles - n16) * PROJ_TILE), F32),
                   jax.ShapeDtypeStruct((t, GATE_PAD), F32)],
        scratch_shapes=[pltpu.VMEM((ROW_TILE, d), BF16)],
        compiler_params=_cparams(("parallel", "arbitrary")),
        name="inproj",
    )(h, modtab, g, w_main, w_gate)


def _chunk_block(n_batch, seq, ctx_len, length, direction):
    n_lat = seq // length
    n_ctx = ctx_len // length
    ctx_base = n_batch * seq // length

    def block(b, s):
        if direction == 0:
            return jnp.where(s < n_ctx, ctx_base + b * n_ctx + s, b * n_lat + (s - n_ctx))
        return jnp.where(s < n_ctx, ctx_base + b * n_ctx + (n_ctx - 1 - s),
                         b * n_lat + (n_lat - 1 - (s - n_ctx)))

    return block, n_ctx + n_lat


def _mlstm_kernel(q_ref, k_ref, v_ref, g_ref, *rest, dh, direction, final):
    if final:
        of_ref, og_ref, bias_ref, tri_ref, o_ref, c_ref, n_ref, m_ref = rest
    else:
        bias_ref, tri_ref, o_ref, c_ref, n_ref, m_ref = rest
    length = q_ref.shape[0]

    @pl.when(pl.program_id(1) == 0)
    def _():
        c_ref[...] = jnp.zeros_like(c_ref)
        n_ref[...] = jnp.zeros_like(n_ref)
        m_ref[...] = jnp.zeros_like(m_ref)

    row = lax.broadcasted_iota(jnp.int32, (length, length), 0)
    col = lax.broadcasted_iota(jnp.int32, (length, length), 1)
    mask = (row <= col) if direction == 0 else (row >= col)
    last = length - 1 if direction == 0 else 0
    pre = g_ref[...] + bias_ref[...]
    logf = jnp.minimum(pre, 0.0) - jnp.log1p(jnp.exp(-jnp.abs(pre)))
    b_all = _dot01(tri_ref[...], logf)
    b_all_t = b_all.T
    pre_t = pre.T
    for hh in range(A_HEADS):
        ci = direction * 2 * A_HEADS + hh
        cf = ci + A_HEADS
        sl = slice(hh * dh, (hh + 1) * dh)
        b_row = b_all_t[cf:cf + 1, :]
        r_row = pre_t[ci:ci + 1, :] - b_row
        r_col = pre[:, ci:ci + 1] - b_all[:, cf:cf + 1]
        m_prev = m_ref[hh][:, 0:1]
        c_t = c_ref[hh]
        nvec = n_ref[hh]

        dmat = jnp.where(mask, r_col + b_row, NEG_BIG)
        inter = b_row + m_prev
        m_t = jnp.maximum(inter, jnp.max(dmat, axis=0, keepdims=True))
        w_inter = jnp.exp(inter - m_t)
        q = q_ref[:, sl]
        k = (k_ref[:, sl].astype(F32) * (dh ** -0.5)).astype(BF16)
        v_t = v_ref[:, sl].astype(F32).T
        s_t = _dot_nt(k, q) * jnp.exp(dmat - m_t)
        num_t = w_inter * _dot_nt(c_t.astype(BF16), q) + _dot(v_t.astype(BF16), s_t.astype(BF16))
        n_rows = jnp.broadcast_to(nvec, (16, dh)).astype(BF16)
        den = w_inter * _dot_nt(n_rows, q)[0:1, :] + jnp.sum(s_t, axis=0, keepdims=True)
        out = (num_t / jnp.maximum(jnp.abs(den), jnp.exp(-m_t))).T
        if final:
            o_ref[:, sl] = (_rms(out + of_ref[:, sl]) * _sigmoid(og_ref[:, sl].astype(F32))).astype(BF16)
        else:
            o_ref[:, sl] = out

        b_last = b_all_t[cf:cf + 1, last:last + 1]
        wlog = b_last + r_row
        m_new = jnp.maximum(b_last + m_prev, jnp.max(wlog, axis=-1, keepdims=True))
        decay = jnp.exp(b_last + m_prev - m_new)
        ws = jnp.exp(wlog - m_new)
        c_ref[hh] = decay * c_t + _dot((v_t * ws).astype(BF16), k)
        ws_rows = jnp.broadcast_to(ws, (16, length)).astype(BF16)
        n_ref[hh] = decay * nvec + _dot(ws_rows, k)[0:1, :]
        m_ref[hh] = jnp.broadcast_to(m_new, m_ref.shape[1:])


def _mlstm(p, gates, gate_bias, tri3, n_batch, seq, ctx_len, dh, direction, o_fwd=None):
    t = p.shape[0]
    aw = A_HEADS * dh
    block, steps = _chunk_block(n_batch, seq, ctx_len, LA, direction)
    final = o_fwd is not None
    spec = lambda colblk, width: pl.BlockSpec((LA, width), lambda b, s: (block(b, s), colblk))
    const = lambda shape: pl.BlockSpec(shape, lambda b, s: (0,) * len(shape))
    in_specs = [spec(0, aw), spec(1, aw), spec(2, aw), spec(0, GATE_PAD)]
    args = [p, p, p, gates]
    if final:
        in_specs += [spec(0, aw), spec(3, aw)]
        args += [o_fwd, p]
    in_specs += [const((1, GATE_PAD)), const((LA, A_PARTS * LA))]
    args += [gate_bias, tri3]
    return pl.pallas_call(
        functools.partial(_mlstm_kernel, dh=dh, direction=direction, final=final),
        grid=(n_batch, steps),
        in_specs=in_specs,
        out_specs=spec(0, aw),
        out_shape=jax.ShapeDtypeStruct((t, aw), BF16 if final else F32),
        scratch_shapes=[pltpu.VMEM((A_HEADS, dh, dh), F32),
                        pltpu.VMEM((A_HEADS, 1, dh), F32),
                        pltpu.VMEM((A_HEADS, 1, 128), F32)],
        compiler_params=_cparams(("parallel", "arbitrary")),
        name="mlstm_bwd" if final else "mlstm_fwd",
    )(*args)


def _hgrn_levels(length):
    levels = []
    c = length // 2
    while c >= 1:
        levels.append(c)
        c //= 2
    return levels


def _hgrn_constants(length, direction):
    t = np.arange(length)[:, None]
    u = np.arange(length)[None, :]
    if direction == 0:
        mats = [u <= t, u > t]
    else:
        mats = [u >= t, u < t]
    masks = []
    for c in _hgrn_levels(length):
        blk = t // (2 * c)
        pos = t % (2 * c)
        if direction == 0:
            ref = blk * 2 * c + c - 1
            q_role = pos >= c
            m = np.where(q_role, (u > ref) & (u <= t), (u > t) & (u <= ref))
            keep = (blk == blk.T) & q_role & (pos.T < c)
        else:
            ref = blk * 2 * c + c
            q_role = pos < c
            m = np.where(q_role, (u >= t) & (u < ref), (u >= ref) & (u < t))
            keep = (blk == blk.T) & q_role & (pos.T >= c)
        mats.append(m)
        masks.append(keep)
    m_all = np.concatenate(mats, 0).astype(np.float32)
    return np.concatenate([m_all] * B_PARTS, axis=1), np.stack(masks).astype(np.float32)


def _hgrn_kernel(q_ref, f_ref, v_ref, *rest, direction, final):
    if final:
        of_ref, og_ref, lb_ref, m3_ref, mask_ref, o_ref, st_ref, d_ref = rest
    else:
        lb_ref, m3_ref, mask_ref, o_ref, st_ref, d_ref = rest
    length, width = q_ref.shape
    n_heads = width // B_DK
    levels = _hgrn_levels(length)
    last = length - 1 if direction == 0 else 0

    @pl.when(pl.program_id(1) == 0)
    def _():
        st_ref[...] = jnp.zeros_like(st_ref)

    t_idx = lax.broadcasted_iota(jnp.int32, (length, 1), 0)
    eye = (lax.broadcasted_iota(jnp.int32, (length, length), 0)
           == lax.broadcasted_iota(jnp.int32, (length, length), 1))
    lb = lb_ref[...]
    f = lb + (1.0 - lb) * _sigmoid(f_ref[...])
    kk = 1.0 - f
    qpre = q_ref[...].astype(F32)
    q = qpre * _sigmoid(qpre)
    d_ref[...] = _dot01(m3_ref[...], jnp.log(f) * LOG2E)
    q_in = (q * jnp.exp2(d_ref[0:length, :])).astype(BF16)
    k_st = kk * jnp.exp2(d_ref[length:2 * length, :])
    decay = jnp.exp2(d_ref[last:last + 1, :])
    xs = []
    for li, c in enumerate(levels):
        pos = t_idx % (2 * c)
        q_role = (pos >= c) if direction == 0 else (pos < c)
        e = jnp.exp2(d_ref[(2 + li) * length:(3 + li) * length, :])
        xs.append((jnp.where(q_role, q, kk) * e).astype(BF16))
    qk = q * kk
    v_all = v_ref[...]
    for hd in range(n_heads):
        sl = slice(hd * B_DK, (hd + 1) * B_DK)
        att = jnp.where(eye, jnp.sum(qk[:, sl], axis=-1, keepdims=True), 0.0)
        for li in range(len(levels)):
            x = xs[li][:, sl]
            att = att + mask_ref[li] * _dot_nt(x, x)
        vh = v_all[:, sl]
        st = st_ref[hd]
        out = _dot(att.astype(BF16), vh) + _dot_nt(q_in[:, sl], st.astype(BF16))
        if final:
            og = og_ref[:, sl].astype(F32)
            o_ref[:, sl] = (_rms(out + of_ref[:, sl]) * (og * _sigmoid(og))).astype(BF16)
        else:
            o_ref[:, sl] = out
        st_ref[hd] = st * decay[:, sl] + _dot(vh.astype(F32).T.astype(BF16), k_st[:, sl].astype(BF16))


def _hgrn(p, pf, lb, consts, n_batch, seq, ctx_len, col0, direction, o_fwd=None):
    t = p.shape[0]
    bw = lb.shape[1]
    m3, masks = consts
    n_rows = m3.shape[0]
    n_lev = masks.shape[0]
    block, steps = _chunk_block(n_batch, seq, ctx_len, LB, direction)
    final = o_fwd is not None
    spec = lambda colblk: pl.BlockSpec((LB, bw), lambda b, s: (block(b, s), colblk))
    const = lambda shape: pl.BlockSpec(shape, lambda b, s: (0,) * len(shape))
    in_specs = [spec(col0), spec(direction), spec(col0 + 1)]
    args = [p, pf, p]
    if final:
        in_specs += [spec(0), spec(col0 + 2)]
        args += [o_fwd, p]
    in_specs += [const((1, bw)), const((n_rows, B_PARTS * LB)), const((n_lev, LB, LB))]
    args += [lb, m3, masks]
    return pl.pallas_call(
        functools.partial(_hgrn_kernel, direction=direction, final=final),
        grid=(n_batch, steps),
        in_specs=in_specs,
        out_specs=spec(0),
        out_shape=jax.ShapeDtypeStruct((t, bw), BF16 if final else F32),
        scratch_shapes=[pltpu.VMEM((bw // B_DK, B_DK, B_DK), F32),
                        pltpu.VMEM((n_rows, bw), F32)],
        compiler_params=_cparams(("parallel", "arbitrary")),
        name="hgrn_bwd" if final else "hgrn_fwd",
    )(*args)


def _scan_rows(a, h, reverse):
    n = a.shape[0]
    idx = lax.broadcasted_iota(jnp.int32, (n, 1), 0)
    k = 1
    while k < n:
        if reverse:
            a_s, h_s, valid = pltpu.roll(a, n - k, 0), pltpu.roll(h, n - k, 0), idx < n - k
        else:
            a_s, h_s, valid = pltpu.roll(a, k, 0), pltpu.roll(h, k, 0), idx >= k
        h = jnp.where(valid, a * h_s + h, h)
        a = jnp.where(valid, a_s * a, a)
        k *= 2
    return a, h


def _shift_rows(x, delta):
    n = x.shape[0]
    idx = lax.broadcasted_iota(jnp.int32, (n, 1), 0)
    if delta == 0:
        return x
    y = pltpu.roll(x, (-delta) % n, 0)
    valid = (idx + delta >= 0) & (idx + delta < n)
    return jnp.where(valid, y, 0.0)


def _sigmoid_t(x):
    return 0.5 * jnp.tanh(0.5 * x) + 0.5


def _rg_gates(y, wg, gb, rate):
    c = y.shape[1]
    g = _dot(y.astype(BF16), wg) + gb
    out = []
    for d in range(2):
        r = _sigmoid_t(g[:, 2 * d * c:(2 * d + 1) * c])
        i = _sigmoid_t(g[:, (2 * d + 1) * c:(2 * d + 2) * c])
        a = jnp.exp2(r * rate[d:d + 1, :])
        out.append((a, jnp.sqrt(1.0 - a * a) * (i * y)))
    return out


def _rglru_kernel(xl_ref, xc_ref, gl_ref, gc_ref, cw_ref, cb_ref, wg_ref, gb_ref, lam_ref,
                  ol_ref, oc_ref, xe_ref, a_ref, bx_ref, hf_ref, *, n_rows):
    w = GRID_W
    seq, c = xl_ref.shape
    rate = (-RG_C * LOG2E) * _softplus(-lam_ref[...])
    cw = cw_ref[...]
    cb = cb_ref[...]
    wg = wg_ref[...]
    gb = gb_ref[...]

    xc = xc_ref[...].astype(F32)
    yc = cb + sum(cw[j:j + 1, :] * _shift_rows(xc, j - 2) for j in range(CONV_W))
    (a_f, bx_f), (a_b, bx_b) = _rg_gates(yc, wg, gb, rate)
    _, hs_f = _scan_rows(a_f, bx_f, reverse=False)
    _, hs_b = _scan_rows(a_b, bx_b, reverse=True)
    n_ctx = xc.shape[0]
    h0 = (hs_f[n_ctx - 1:n_ctx, :], hs_b[0:1, :])
    oc_ref[...] = ((hs_f + hs_b) * _gelu_tanh(gc_ref[...].astype(F32))).astype(BF16)

    xe_ref[2 * w:2 * w + seq, :] = xl_ref[...].astype(F32)
    xe_ref[0:w, :] = _shift_rows(xl_ref[(n_rows - 2) * w:(n_rows - 1) * w, :].astype(F32), -1)
    xe_ref[w:2 * w, :] = _shift_rows(xl_ref[(n_rows - 1) * w:n_rows * w, :].astype(F32), -1)
    xe_ref[2 * w + seq:3 * w + seq, :] = _shift_rows(xl_ref[0:w, :].astype(F32), 1)

    def gate_chunk(i, carry):
        base = pl.multiple_of(i * RG_ROWS, RG_ROWS)
        y = cb + sum(cw[j:j + 1, :] * xe_ref[pl.ds(base + j * w, RG_ROWS), :] for j in range(CONV_W))
        for d, (a, bx) in enumerate(_rg_gates(y, wg, gb, rate)):
            a_ref[d, pl.ds(base, RG_ROWS), :] = a
            bx_ref[d, pl.ds(base, RG_ROWS), :] = bx
        return carry

    lax.fori_loop(0, seq // RG_ROWS, gate_chunk, 0)

    for d in range(2):
        def slab(i):
            r = i if d == 0 else n_rows - 1 - i
            return pl.ds(pl.multiple_of(r * w, w), w)

        def column_totals(i, carry):
            a_tot, h_end = carry
            a = a_ref[d, slab(i), :]
            return a * a_tot, a * h_end + bx_ref[d, slab(i), :]

        a_tot, h_end = lax.fori_loop(0, n_rows, column_totals,
                                     (jnp.ones((w, c), F32), jnp.zeros((w, c), F32)))
        a_cum, h_cum = _scan_rows(a_tot, h_end, reverse=(d == 1))
        after = a_cum * h0[d] + h_cum
        w_idx = lax.broadcasted_iota(jnp.int32, (w, 1), 0)
        if d == 0:
            h_in = jnp.where(w_idx == 0, h0[d], pltpu.roll(after, 1, 0))
        else:
            h_in = jnp.where(w_idx == w - 1, h0[d], pltpu.roll(after, w - 1, 0))

        def emit(i, h):
            h = a_ref[d, slab(i), :] * h + bx_ref[d, slab(i), :]
            if d == 0:
                hf_ref[slab(i), :] = h
            else:
                gate = _gelu_tanh(gl_ref[slab(i), :].astype(F32))
                ol_ref[slab(i), :] = ((hf_ref[slab(i), :] + h) * gate).astype(BF16)
            return h

        lax.fori_loop(0, n_rows, emit, h_in)


def _rglru(p, conv_w, conv_b, wg, gb, lam, n_batch, seq, ctx_len, col0):
    cw_total = conv_w.shape[1]
    n_slabs = cw_total // RG_SLAB
    n_rows = seq // GRID_W
    ctx_base = n_batch * seq // ctx_len
    lat = lambda colblk: pl.BlockSpec((seq, RG_SLAB), lambda b, j: (b, colblk * n_slabs + j))
    ctx = lambda colblk: pl.BlockSpec((ctx_len, RG_SLAB), lambda b, j: (ctx_base + b, colblk * n_slabs + j))
    return pl.pallas_call(
        functools.partial(_rglru_kernel, n_rows=n_rows),
        grid=(n_batch, n_slabs),
        in_specs=[
            lat(col0), ctx(col0), lat(col0 + 1), ctx(col0 + 1),
            pl.BlockSpec((CONV_W, RG_SLAB), lambda b, j: (0, j)),
            pl.BlockSpec((1, RG_SLAB), lambda b, j: (0, j)),
            pl.BlockSpec((None, RG_SLAB, 4 * RG_SLAB), lambda b, j: (j, 0, 0)),
            pl.BlockSpec((None, 1, 4 * RG_SLAB), lambda b, j: (j, 0, 0)),
            pl.BlockSpec((2, RG_SLAB), lambda b, j: (0, j)),
        ],
        out_specs=[pl.BlockSpec((seq, RG_SLAB), lambda b, j: (b, j)),
                   pl.BlockSpec((ctx_len, RG_SLAB), lambda b, j: (b, j))],
        out_shape=[jax.ShapeDtypeStruct((n_batch * seq, cw_total), BF16),
                   jax.ShapeDtypeStruct((n_batch * ctx_len, cw_total), BF16)],
        scratch_shapes=[pltpu.VMEM((seq + 3 * GRID_W, RG_SLAB), F32),
                        pltpu.VMEM((2, seq, RG_SLAB), F32), pltpu.VMEM((2, seq, RG_SLAB), F32),
                        pltpu.VMEM((seq, RG_SLAB), F32)],
        compiler_params=_cparams(("parallel", "arbitrary")),
        name="rglru",
    )(p, p, p, p, conv_w, conv_b, wg, gb, lam)


def _merge_kernel(ya_ref, yb_ref, ycl_ref, ycc_ref, ga_ref, gb_ref, gc_ref, h_ref, mod_ref, bw_ref,
                  wo_ref, o_ref, *, mod_base, n_lat_tiles):
    j = pl.program_id(1)

    @pl.when(j == 0)
    def _():
        o_ref[...] = jnp.zeros_like(o_ref)

    yc = jnp.where(pl.program_id(0) < n_lat_tiles, ycl_ref[...], ycc_ref[...])
    merged = (_sigmoid(ga_ref[...].astype(F32)) * _dot(ya_ref[...], bw_ref[0])
              + _sigmoid(gb_ref[...].astype(F32)) * _dot(yb_ref[...], bw_ref[1])
              + _sigmoid(gc_ref[...].astype(F32)) * _dot(yc, bw_ref[2]))
    o_ref[...] += _dot(merged.astype(BF16), wo_ref[...])

    @pl.when(j == pl.num_programs(1) - 1)
    def _():
        o_ref[...] = h_ref[...] + mod_ref[mod_base:mod_base + 1, :] * o_ref[...]


def _merge(h, n_rows, p, ya, yb, yc_lat, yc_ctx, modtab, mod_row, branch_w, w_out, layer, mod_base):
    d = h.shape[1]
    bw = ya.shape[1]
    n_j = d // MERGE_TILE
    gate0 = 9 * bw // MERGE_TILE
    per_branch = d // MERGE_TILE
    n_lat_tiles = yc_lat.shape[0] // MERGE_ROWS
    branch = pl.BlockSpec((MERGE_ROWS, bw), lambda i, j: (i, 0))
    branch_lat = pl.BlockSpec((MERGE_ROWS, bw), lambda i, j: (jnp.minimum(i, n_lat_tiles - 1), 0))
    branch_ctx = pl.BlockSpec((MERGE_ROWS, bw), lambda i, j: (jnp.maximum(i - n_lat_tiles, 0), 0))
    gate = lambda k: pl.BlockSpec((MERGE_ROWS, MERGE_TILE), lambda i, j: (i, gate0 + k * per_branch + j))
    return pl.pallas_call(
        functools.partial(_merge_kernel, mod_base=mod_base, n_lat_tiles=n_lat_tiles),
        grid=(n_rows // MERGE_ROWS, n_j),
        in_specs=[branch, branch, branch_lat, branch_ctx, gate(0), gate(1), gate(2),
                  pl.BlockSpec((MERGE_ROWS, d), lambda i, j: (i, 0)),
                  pl.BlockSpec((None, N_MOD, d), lambda i, j: (mod_row(MERGE_ROWS)(i), 0, 0)),
                  pl.BlockSpec((None, 3, bw, MERGE_TILE), lambda i, j: (layer, 0, 0, j)),
                  pl.BlockSpec((None, MERGE_TILE, d), lambda i, j: (layer, j, 0))],
        out_specs=pl.BlockSpec((MERGE_ROWS, d), lambda i, j: (i, 0)),
        out_shape=jax.ShapeDtypeStruct((n_rows, d), F32),
        compiler_params=_cparams(("parallel", "arbitrary")),
        name="merge",
    )(ya, yb, yc_lat, yc_ctx, p, p, p, h, modtab, branch_w, w_out)


def kernel(x, c, ctx, c_ctx, mod_w, mod_b, norm_g, ffn_w_in, ffn_w_out, w_in, mlstm_gate_b,
           hgrn_lb_logits, conv_w, conv_b, rg_gate_w, rg_gate_b, rg_lambda, branch_w, w_out, final_g):
    n_batch, seq, d = x.shape
    ctx_len = ctx.shape[1]
    depth = mod_w.shape[0]
    aw = branch_w.shape[2]
    dh = aw // A_HEADS
    n_lat_rows = n_batch * seq
    n_rows = n_lat_rows + n_batch * ctx_len
    assert seq % ROW_TILE == 0 and (n_batch * ctx_len) % ROW_TILE == 0
    assert seq % LA == 0 and ctx_len % LA == 0 and seq % LB == 0 and ctx_len % LB == 0
    assert seq % GRID_W == 0 and seq % RG_ROWS == 0 and n_lat_rows % ctx_len == 0
    assert n_batch + 1 <= MOD_ROWS and aw % RG_SLAB == 0 and RG_SLAB % (aw // C_BLOCKS) == 0

    def mod_row(tile):
        return lambda i: jnp.where(i < n_lat_rows // tile, 1 + i // (seq // tile), 0)

    c16 = jnp.zeros((MOD_ROWS, d), F32).at[0].set(c_ctx).at[1:1 + n_batch].set(c)
    modtab = _mod_table(c16, mod_w, mod_b).reshape(depth, MOD_ROWS, N_MOD, d)
    ffn_w_in_b = ffn_w_in.astype(BF16)
    ffn_w_out_b = ffn_w_out.astype(BF16)
    g0 = 4 * aw
    n_gate = 4 * A_HEADS
    b0 = g0 + n_gate
    w_in_b = w_in.astype(BF16)
    w_main = jnp.concatenate([w_in_b[:, :, :g0], w_in_b[:, :, b0:b0 + aw], w_in_b[:, :, b0 + 3 * aw:],
                              w_in_b[:, :, b0 + aw:b0 + 3 * aw]], axis=-1)
    n16 = (w_main.shape[2] - 2 * aw) // PROJ_TILE
    w_gate = jnp.pad(w_in_b[:, :, g0:g0 + n_gate], ((0, 0), (0, 0), (0, GATE_PAD - n_gate)))
    gate_bias = jnp.pad(mlstm_gate_b.reshape(depth, 1, n_gate), ((0, 0), (0, 0), (0, GATE_PAD - n_gate)))
    lb_p = jax.nn.softmax(hgrn_lb_logits.astype(F32), axis=0)
    lb_all = jnp.cumsum(lb_p, axis=0) - lb_p[0:1]
    branch_w_b = branch_w.astype(BF16)
    w_out_b = w_out.astype(BF16)
    n_slabs = aw // RG_SLAB
    c_db = aw // C_BLOCKS
    per_slab = RG_SLAB // c_db
    blocks = rg_gate_w.reshape(depth, 2, 2, n_slabs, per_slab, c_db, c_db)
    eye = jnp.eye(per_slab, dtype=F32)
    dense = jnp.einsum('ldgspio,pq->ldgspiqo', blocks, eye).reshape(depth, 2, 2, n_slabs, RG_SLAB, RG_SLAB)
    rg_w = dense.transpose(0, 3, 4, 1, 2, 5).reshape(depth, n_slabs, RG_SLAB, 4 * RG_SLAB).astype(BF16)
    rg_b = (rg_gate_b.reshape(depth, 2, 2, n_slabs, RG_SLAB).transpose(0, 3, 1, 2, 4)
            .reshape(depth, n_slabs, 1, 4 * RG_SLAB))
    tri = np.tril(np.ones((LA, LA), np.float32))
    tri3 = [jnp.asarray(np.concatenate([m] * A_PARTS, axis=1), BF16) for m in (tri, tri.T)]
    hgrn_consts = []
    for direction in (0, 1):
        m3, masks = _hgrn_constants(LB, direction)
        hgrn_consts.append((jnp.asarray(m3, BF16), jnp.asarray(masks, F32)))

    h = jnp.concatenate([x.reshape(n_lat_rows, d), ctx.reshape(n_batch * ctx_len, d)], axis=0)
    for layer in range(depth):
        last = layer == depth - 1
        mt = modtab[layer]
        ng = norm_g[layer]
        h = _ffn(h, n_rows, mt, mod_row, ng[0:1], ffn_w_in_b, ffn_w_out_b, layer, 0, 0)
        p, pf, gates = _inproj(h, mt, mod_row, ng[1:2], w_main, w_gate, layer, 3, n16)
        a_fwd = _mlstm(p, gates, gate_bias[layer], tri3[0], n_batch, seq, ctx_len, dh, 0)
        ya = _mlstm(p, gates, gate_bias[layer], tri3[1], n_batch, seq, ctx_len, dh, 1, o_fwd=a_fwd)
        b_fwd = _hgrn(p, pf, lb_all[layer, 0:1], hgrn_consts[0], n_batch, seq, ctx_len, 4, 0)
        yb = _hgrn(p, pf, lb_all[layer, 1:2], hgrn_consts[1], n_batch, seq, ctx_len, 4, 1, o_fwd=b_fwd)
        yc_lat, yc_ctx = _rglru(p, conv_w[layer], conv_b[layer][None, :], rg_w[layer], rg_b[layer],
                                rg_lambda[layer], n_batch, seq, ctx_len, 7)
        rows_out = n_lat_rows if last else n_rows
        h = _merge(h, rows_out, p, ya, yb, yc_lat, yc_ctx, mt, mod_row, branch_w_b, w_out_b, layer, 5)
        h = _ffn(h, rows_out, mt, mod_row, ng[2:3], ffn_w_in_b, ffn_w_out_b, layer, 1, 6,
                 final_g=final_g[None, :] if last else None)
    return h.reshape(n_batch, seq, d)
```

```python
import functools

import numpy as np
import jax
import jax.numpy as jnp
from jax import lax
from jax.experimental import pallas as pl
from jax.experimental.pallas import tpu as pltpu

EPS = 1e-6
N_MOD = 9
GRID_W = 64
A_HEADS = 4
B_DK = 128
C_BLOCKS = 16
CONV_W = 4
RG_C = 8.0
NEG_BIG = -1e30
LOG2E = 1.4426950408889634

MOD_ROWS = 16
GATE_PAD = 128
ROW_TILE = 1024
FF_TILE = 512
FFN_SUB = 512
PROJ_TILE = 1024
MOD_TILE = 1024
MERGE_TILE = 1024
MERGE_ROWS = 512
LA = 256
LB = 128
A_PARTS = 3
B_PARTS = 2
HGRN_GROUP = 8
RG_SLAB = 256
RG_ROWS = 512
VMEM_LIMIT = 60 * 1024 * 1024

F32 = jnp.float32
BF16 = jnp.bfloat16


def _cparams(sem):
    return pltpu.CompilerParams(dimension_semantics=sem, vmem_limit_bytes=VMEM_LIMIT)


def _dot(a, b):
    return jnp.dot(a, b, preferred_element_type=F32)


def _dot_nt(a, b):
    return lax.dot_general(a, b, (((1,), (1,)), ((), ())), preferred_element_type=F32)


def _dot01(m_rep, x):
    parts = m_rep.shape[1] // x.shape[0]
    pieces = []
    rest = x
    for _ in range(parts):
        piece = rest.astype(BF16)
        pieces.append(piece)
        rest = rest - piece.astype(F32)
    return _dot(m_rep, jnp.concatenate(pieces, axis=0))


def _sigmoid(x):
    return 0.5 * jnp.tanh(0.5 * x) + 0.5


def _softplus(z):
    return jnp.maximum(z, 0.0) + jnp.log1p(jnp.exp(-jnp.abs(z)))


def _gelu_tanh(x):
    return 0.5 * x * (1.0 + jnp.tanh(np.sqrt(2.0 / np.pi).astype(np.float32) * (x + 0.044715 * (x * x * x))))


def _rms(x):
    return x * lax.rsqrt(jnp.mean(x * x, axis=-1, keepdims=True) + EPS)


def _adaln(h, g, shift, scale):
    return _rms(h) * (g * (1.0 + scale)) + shift


def _mod_kernel(c_ref, w_ref, b_ref, o_ref):
    cv = c_ref[...]
    sc = (cv * _sigmoid(cv)).astype(BF16)
    o_ref[...] = _dot(sc, w_ref[...].astype(BF16)) + b_ref[...]


def _mod_table(c16, mod_w, mod_b):
    depth, d, n = mod_w.shape
    return pl.pallas_call(
        _mod_kernel,
        grid=(depth, n // MOD_TILE),
        in_specs=[
            pl.BlockSpec((MOD_ROWS, d), lambda l, j: (0, 0)),
            pl.BlockSpec((None, d, MOD_TILE), lambda l, j: (l, 0, j)),
            pl.BlockSpec((None, 1, MOD_TILE), lambda l, j: (l, 0, j)),
        ],
        out_specs=pl.BlockSpec((None, MOD_ROWS, MOD_TILE), lambda l, j: (l, 0, j)),
        out_shape=jax.ShapeDtypeStruct((depth, MOD_ROWS, n), F32),
        compiler_params=_cparams(("arbitrary", "arbitrary")),
        name="mod_table",
    )(c16, mod_w, mod_b.reshape(depth, 1, n))


def _ffn_kernel(h_ref, mod_ref, g_ref, wg_ref, wu_ref, wo_ref, *rest, mod_base, final):
    if final:
        fg_ref, o_ref, xn_ref = rest
    else:
        o_ref, xn_ref = rest
    j = pl.program_id(1)
    subs = [slice(r, r + FFN_SUB) for r in range(0, h_ref.shape[0], FFN_SUB)]

    @pl.when(j == 0)
    def _():
        for rows in subs:
            xn = _adaln(h_ref[rows, :], g_ref[...], mod_ref[mod_base:mod_base + 1, :],
                        mod_ref[mod_base + 1:mod_base + 2, :])
            xn_ref[rows, :] = xn.astype(BF16)
            o_ref[rows, :] = jnp.zeros((FFN_SUB, o_ref.shape[1]), F32)

    for rows in subs:
        xn = xn_ref[rows, :]
        gate = _dot(xn, wg_ref[...])
        up = _dot(xn, wu_ref[...])
        act = (gate * _sigmoid(gate)) * up
        o_ref[rows, :] += _dot(act.astype(BF16), wo_ref[...])

    @pl.when(j == pl.num_programs(1) - 1)
    def _():
        for rows in subs:
            out = h_ref[rows, :] + (0.5 * mod_ref[mod_base + 2:mod_base + 3, :]) * o_ref[rows, :]
            if final:
                out = _rms(out) * fg_ref[...]
            o_ref[rows, :] = out


def _ffn(h, n_rows, modtab, mod_row, g, w_in, w_out, layer, which, mod_base, final_g=None):
    d = h.shape[1]
    d_ff = w_out.shape[2]
    n_ff = d_ff // FF_TILE
    final = final_g is not None
    in_specs = [
        pl.BlockSpec((ROW_TILE, d), lambda i, j: (i, 0)),
        pl.BlockSpec((None, N_MOD, d), lambda i, j: (mod_row(ROW_TILE)(i), 0, 0)),
        pl.BlockSpec((1, d), lambda i, j: (0, 0)),
        pl.BlockSpec((None, None, d, FF_TILE), lambda i, j: (layer, which, 0, j)),
        pl.BlockSpec((None, None, d, FF_TILE), lambda i, j: (layer, which, 0, j + n_ff)),
        pl.BlockSpec((None, None, FF_TILE, d), lambda i, j: (layer, which, j, 0)),
    ]
    args = [h, modtab, g, w_in, w_in, w_out]
    if final:
        in_specs.append(pl.BlockSpec((1, d), lambda i, j: (0, 0)))
        args.append(final_g)
    return pl.pallas_call(
        functools.partial(_ffn_kernel, mod_base=mod_base, final=final),
        grid=(n_rows // ROW_TILE, n_ff),
        in_specs=in_specs,
        out_specs=pl.BlockSpec((ROW_TILE, d), lambda i, j: (i, 0)),
        out_shape=jax.ShapeDtypeStruct((n_rows, d), F32),
        scratch_shapes=[pltpu.VMEM((ROW_TILE, d), BF16)],
        compiler_params=_cparams(("parallel", "arbitrary")),
        name="ffn",
    )(*args)


def _inproj_kernel(h_ref, mod_ref, g_ref, w_ref, wgate_ref, p16_ref, pf_ref, gate_ref, xn_ref,
                   *, mod_base, n16):
    j = pl.program_id(1)

    @pl.when(j == 0)
    def _():
        xn = _adaln(h_ref[...], g_ref[...], mod_ref[mod_base:mod_base + 1, :],
                    mod_ref[mod_base + 1:mod_base + 2, :]).astype(BF16)
        xn_ref[...] = xn
        gate_ref[...] = _dot(xn, wgate_ref[...])

    @pl.when(j < n16)
    def _():
        p16_ref[...] = _dot(xn_ref[...], w_ref[...]).astype(BF16)

    @pl.when(j >= n16)
    def _():
        pf_ref[...] = _dot(xn_ref[...], w_ref[...])


def _inproj(h, modtab, mod_row, g, w_main, w_gate, layer, mod_base, n16):
    t, d = h.shape
    n = w_main.shape[2]
    n_tiles = n // PROJ_TILE
    return pl.pallas_call(
        functools.partial(_inproj_kernel, mod_base=mod_base, n16=n16),
        grid=(t // ROW_TILE, n_tiles),
        in_specs=[
            pl.BlockSpec((ROW_TILE, d), lambda i, j: (i, 0)),
            pl.BlockSpec((None, N_MOD, d), lambda i, j: (mod_row(ROW_TILE)(i), 0, 0)),
            pl.BlockSpec((1, d), lambda i, j: (0, 0)),
            pl.BlockSpec((None, d, PROJ_TILE), lambda i, j: (layer, 0, j)),
            pl.BlockSpec((None, d, GATE_PAD), lambda i, j: (layer, 0, 0)),
        ],
        out_specs=[
            pl.BlockSpec((ROW_TILE, PROJ_TILE), lambda i, j: (i, jnp.minimum(j, n16 - 1))),
            pl.BlockSpec((ROW_TILE, PROJ_TILE), lambda i, j: (i, jnp.maximum(j - n16, 0))),
            pl.BlockSpec((ROW_TILE, GATE_PAD), lambda i, j: (i, 0)),
        ],
        out_shape=[jax.ShapeDtypeStruct((t, n16 * PROJ_TILE), BF16),
                   jax.ShapeDtypeStruct((t, (n_tiles - n16) * PROJ_TILE), F32),
                   jax.ShapeDtypeStruct((t, GATE_PAD), F32)],
        scratch_shapes=[pltpu.VMEM((ROW_TILE, d), BF16)],
        compiler_params=_cparams(("parallel", "arbitrary")),
        name="inproj",
    )(h, modtab, g, w_main, w_gate)


def _chunk_block(n_batch, seq, ctx_len, length, direction):
    n_lat = seq // length
    n_ctx = ctx_len // length
    ctx_base = n_batch * seq // length

    def block(b, s):
        if direction == 0:
            return jnp.where(s < n_ctx, ctx_base + b * n_ctx + s, b * n_lat + (s - n_ctx))
        return jnp.where(s < n_ctx, ctx_base + b * n_ctx + (n_ctx - 1 - s),
                         b * n_lat + (n_lat - 1 - (s - n_ctx)))

    return block, n_ctx + n_lat


def _mlstm_kernel(q_ref, k_ref, v_ref, g_ref, *rest, dh, direction, final):
    if final:
        of_ref, og_ref, bias_ref, tri_ref, o_ref, c_ref, n_ref, m_ref = rest
    else:
        bias_ref, tri_ref, o_ref, c_ref, n_ref, m_ref = rest
    length = q_ref.shape[0]

    @pl.when(pl.program_id(1) == 0)
    def _():
        c_ref[...] = jnp.zeros_like(c_ref)
        n_ref[...] = jnp.zeros_like(n_ref)
        m_ref[...] = jnp.zeros_like(m_ref)

    row = lax.broadcasted_iota(jnp.int32, (length, length), 0)
    col = lax.broadcasted_iota(jnp.int32, (length, length), 1)
    mask = (row <= col) if direction == 0 else (row >= col)
    last = length - 1 if direction == 0 else 0
    pre = g_ref[...] + bias_ref[...]
    logf = jnp.minimum(pre, 0.0) - jnp.log1p(jnp.exp(-jnp.abs(pre)))
    b_all = _dot01(tri_ref[...], logf)
    b_all_t = b_all.T
    pre_t = pre.T
    for hh in range(A_HEADS):
        ci = direction * 2 * A_HEADS + hh
        cf = ci + A_HEADS
        sl = slice(hh * dh, (hh + 1) * dh)
        b_row = b_all_t[cf:cf + 1, :]
        r_row = pre_t[ci:ci + 1, :] - b_row
        r_col = pre[:, ci:ci + 1] - b_all[:, cf:cf + 1]
        m_prev = m_ref[hh][:, 0:1]
        c_t = c_ref[hh]
        nvec = n_ref[hh]

        dmat = jnp.where(mask, r_col + b_row, NEG_BIG)
        inter = b_row + m_prev
        m_t = jnp.maximum(inter, jnp.max(dmat, axis=0, keepdims=True))
        w_inter = jnp.exp(inter - m_t)
        q = q_ref[:, sl]
        k = (k_ref[:, sl].astype(F32) * (dh ** -0.5)).astype(BF16)
        v_t = v_ref[:, sl].astype(F32).T
        s_t = _dot_nt(k, q) * jnp.exp(dmat - m_t)
        num_t = w_inter * _dot_nt(c_t.astype(BF16), q) + _dot(v_t.astype(BF16), s_t.astype(BF16))
        n_rows = jnp.broadcast_to(nvec, (16, dh)).astype(BF16)
        den = w_inter * _dot_nt(n_rows, q)[0:1, :] + jnp.sum(s_t, axis=0, keepdims=True)
        out = (num_t / jnp.maximum(jnp.abs(den), jnp.exp(-m_t))).T
        if final:
            o_ref[:, sl] = (_rms(out + of_ref[:, sl]) * _sigmoid(og_ref[:, sl].astype(F32))).astype(BF16)
        else:
            o_ref[:, sl] = out

        b_last = b_all_t[cf:cf + 1, last:last + 1]
        wlog = b_last + r_row
        m_new = jnp.maximum(b_last + m_prev, jnp.max(wlog, axis=-1, keepdims=True))
        decay = jnp.exp(b_last + m_prev - m_new)
        ws = jnp.exp(wlog - m_new)
        c_ref[hh] = decay * c_t + _dot((v_t * ws).astype(BF16), k)
        ws_rows = jnp.broadcast_to(ws, (16, length)).astype(BF16)
        n_ref[hh] = decay * nvec + _dot(ws_rows, k)[0:1, :]
        m_ref[hh] = jnp.broadcast_to(m_new, m_ref.shape[1:])


def _mlstm(p, gates, gate_bias, tri3, n_batch, seq, ctx_len, dh, direction, o_fwd=None):
    t = p.shape[0]
    aw = A_HEADS * dh
    block, steps = _chunk_block(n_batch, seq, ctx_len, LA, direction)
    final = o_fwd is not None
    spec = lambda colblk, width: pl.BlockSpec((LA, width), lambda b, s: (block(b, s), colblk))
    const = lambda shape: pl.BlockSpec(shape, lambda b, s: (0,) * len(shape))
    in_specs = [spec(0, aw), spec(1, aw), spec(2, aw), spec(0, GATE_PAD)]
    args = [p, p, p, gates]
    if final:
        in_specs += [spec(0, aw), spec(3, aw)]
        args += [o_fwd, p]
    in_specs += [const((1, GATE_PAD)), const((LA, A_PARTS * LA))]
    args += [gate_bias, tri3]
    return pl.pallas_call(
        functools.partial(_mlstm_kernel, dh=dh, direction=direction, final=final),
        grid=(n_batch, steps),
        in_specs=in_specs,
        out_specs=spec(0, aw),
        out_shape=jax.ShapeDtypeStruct((t, aw), BF16 if final else F32),
        scratch_shapes=[pltpu.VMEM((A_HEADS, dh, dh), F32),
                        pltpu.VMEM((A_HEADS, 1, dh), F32),
                        pltpu.VMEM((A_HEADS, 1, 128), F32)],
        compiler_params=_cparams(("parallel", "arbitrary")),
        name="mlstm_bwd" if final else "mlstm_fwd",
    )(*args)


def _hgrn_levels(length):
    levels = []
    c = length // 2
    while c >= 1:
        levels.append(c)
        c //= 2
    return levels


def _hgrn_constants(length, direction):
    t = np.arange(length)[:, None]
    u = np.arange(length)[None, :]
    if direction == 0:
        mats = [u <= t, u > t]
    else:
        mats = [u >= t, u < t]
    masks = []
    for c in _hgrn_levels(length):
        blk = t // (2 * c)
        pos = t % (2 * c)
        if direction == 0:
            ref = blk * 2 * c + c - 1
            q_role = pos >= c
            m = np.where(q_role, (u > ref) & (u <= t), (u > t) & (u <= ref))
            keep = (blk == blk.T) & q_role & (pos.T < c)
        else:
            ref = blk * 2 * c + c
            q_role = pos < c
            m = np.where(q_role, (u >= t) & (u < ref), (u >= ref) & (u < t))
            keep = (blk == blk.T) & q_role & (pos.T >= c)
        mats.append(m)
        masks.append(keep)
    m_all = np.concatenate(mats, 0).astype(np.float32)
    return np.concatenate([m_all] * B_PARTS, axis=1), np.stack(masks).astype(np.float32)


def _hgrn_kernel(q_ref, f_ref, v_ref, *rest, direction, final):
    if final:
        of_ref, og_ref, lb_ref, m3_ref, mask_ref, o_ref, st_ref, d_ref = rest
    else:
        lb_ref, m3_ref, mask_ref, o_ref, st_ref, d_ref = rest
    length, width = q_ref.shape
    n_heads = width // B_DK
    levels = _hgrn_levels(length)
    last = length - 1 if direction == 0 else 0

    @pl.when(pl.program_id(1) == 0)
    def _():
        st_ref[...] = jnp.zeros_like(st_ref)

    t_idx = lax.broadcasted_iota(jnp.int32, (length, 1), 0)
    eye = (lax.broadcasted_iota(jnp.int32, (length, length), 0)
           == lax.broadcasted_iota(jnp.int32, (length, length), 1))
    gw = HGRN_GROUP * B_DK
    for g in range(n_heads // HGRN_GROUP):
        cols = slice(g * gw, (g + 1) * gw)
        lb = lb_ref[:, cols]
        f = lb + (1.0 - lb) * _sigmoid(f_ref[:, cols])
        kk = 1.0 - f
        qpre = q_ref[:, cols].astype(F32)
        q = qpre * _sigmoid(qpre)
        d_ref[:, cols] = _dot01(m3_ref[...], jnp.log(f) * LOG2E)
        q_in = (q * jnp.exp2(d_ref[0:length, cols])).astype(BF16)
        k_st = (kk * jnp.exp2(d_ref[length:2 * length, cols])).astype(BF16)
        decay = jnp.exp2(d_ref[last:last + 1, cols])
        xs = []
        for li, c in enumerate(levels):
            pos = t_idx % (2 * c)
            q_role = (pos >= c) if direction == 0 else (pos < c)
            e = jnp.exp2(d_ref[(2 + li) * length:(3 + li) * length, cols])
            xs.append((jnp.where(q_role, q, kk) * e).astype(BF16))
        qk = q * kk
        for hg in range(HGRN_GROUP):
            hd = g * HGRN_GROUP + hg
            sl = slice(hd * B_DK, (hd + 1) * B_DK)
            gl = slice(hg * B_DK, (hg + 1) * B_DK)
            att = jnp.where(eye, jnp.sum(qk[:, gl], axis=-1, keepdims=True), 0.0)
            for li in range(len(levels)):
                x = xs[li][:, gl]
                att = att + mask_ref[li] * _dot_nt(x, x)
            vh = v_ref[:, sl]
            st = st_ref[hd]
            out = _dot(att.astype(BF16), vh) + _dot_nt(q_in[:, gl], st.astype(BF16))
            if final:
                og = og_ref[:, sl].astype(F32)
                o_ref[:, sl] = (_rms(out + of_ref[:, sl]) * (og * _sigmoid(og))).astype(BF16)
            else:
                o_ref[:, sl] = out
            st_ref[hd] = st * decay[:, gl] + _dot(vh.astype(F32).T.astype(BF16), k_st[:, gl])


def _hgrn(p, pf, lb, consts, n_batch, seq, ctx_len, col0, direction, o_fwd=None):
    t = p.shape[0]
    bw = lb.shape[1]
    m3, masks = consts
    n_rows = m3.shape[0]
    n_lev = masks.shape[0]
    block, steps = _chunk_block(n_batch, seq, ctx_len, LB, direction)
    final = o_fwd is not None
    spec = lambda colblk: pl.BlockSpec((LB, bw), lambda b, s: (block(b, s), colblk))
    const = lambda shape: pl.BlockSpec(shape, lambda b, s: (0,) * len(shape))
    in_specs = [spec(col0), spec(direction), spec(col0 + 1)]
    args = [p, pf, p]
    if final:
        in_specs += [spec(0), spec(col0 + 2)]
        args += [o_fwd, p]
    in_specs += [const((1, bw)), const((n_rows, B_PARTS * LB)), const((n_lev, LB, LB))]
    args += [lb, m3, masks]
    return pl.pallas_call(
        functools.partial(_hgrn_kernel, direction=direction, final=final),
        grid=(n_batch, steps),
        in_specs=in_specs,
        out_specs=spec(0),
        out_shape=jax.ShapeDtypeStruct((t, bw), BF16 if final else F32),
        scratch_shapes=[pltpu.VMEM((bw // B_DK, B_DK, B_DK), F32),
                        pltpu.VMEM((n_rows, bw), F32)],
        compiler_params=_cparams(("parallel", "arbitrary")),
        name="hgrn_bwd" if final else "hgrn_fwd",
    )(*args)


def _scan_rows(a, h, reverse):
    n = a.shape[0]
    idx = lax.broadcasted_iota(jnp.int32, (n, 1), 0)
    k = 1
    while k < n:
        if reverse:
            a_s, h_s, valid = pltpu.roll(a, n - k, 0), pltpu.roll(h, n - k, 0), idx < n - k
        else:
            a_s, h_s, valid = pltpu.roll(a, k, 0), pltpu.roll(h, k, 0), idx >= k
        h = jnp.where(valid, a * h_s + h, h)
        a = jnp.where(valid, a_s * a, a)
        k *= 2
    return a, h


def _shift_rows(x, delta):
    n = x.shape[0]
    idx = lax.broadcasted_iota(jnp.int32, (n, 1), 0)
    if delta == 0:
        return x
    y = pltpu.roll(x, (-delta) % n, 0)
    valid = (idx + delta >= 0) & (idx + delta < n)
    return jnp.where(valid, y, 0.0)


def _rg_gates(y, wg, gb, rate):
    c = y.shape[1]
    g = _dot(y.astype(BF16), wg) + gb
    out = []
    for d in range(2):
        r = _sigmoid(g[:, 2 * d * c:(2 * d + 1) * c])
        i = _sigmoid(g[:, (2 * d + 1) * c:(2 * d + 2) * c])
        a = jnp.exp2(r * rate[d:d + 1, :])
        out.append((a, jnp.sqrt(1.0 - a * a) * (i * y)))
    return out


def _rglru_kernel(xl_ref, xc_ref, gl_ref, gc_ref, cw_ref, cb_ref, wg_ref, gb_ref, lam_ref,
                  ol_ref, oc_ref, xe_ref, a_ref, bx_ref, hf_ref, *, n_rows):
    w = GRID_W
    seq, c = xl_ref.shape
    rate = (-RG_C * LOG2E) * _softplus(-lam_ref[...])
    cw = cw_ref[...]
    cb = cb_ref[...]
    wg = wg_ref[...]
    gb = gb_ref[...]

    xc = xc_ref[...].astype(F32)
    yc = cb + sum(cw[j:j + 1, :] * _shift_rows(xc, j - 2) for j in range(CONV_W))
    (a_f, bx_f), (a_b, bx_b) = _rg_gates(yc, wg, gb, rate)
    _, hs_f = _scan_rows(a_f, bx_f, reverse=False)
    _, hs_b = _scan_rows(a_b, bx_b, reverse=True)
    n_ctx = xc.shape[0]
    h0 = (hs_f[n_ctx - 1:n_ctx, :], hs_b[0:1, :])
    oc_ref[...] = ((hs_f + hs_b) * _gelu_tanh(gc_ref[...].astype(F32))).astype(BF16)

    xe_ref[2 * w:2 * w + seq, :] = xl_ref[...].astype(F32)
    xe_ref[0:w, :] = _shift_rows(xl_ref[(n_rows - 2) * w:(n_rows - 1) * w, :].astype(F32), -1)
    xe_ref[w:2 * w, :] = _shift_rows(xl_ref[(n_rows - 1) * w:n_rows * w, :].astype(F32), -1)
    xe_ref[2 * w + seq:3 * w + seq, :] = _shift_rows(xl_ref[0:w, :].astype(F32), 1)

    def gate_chunk(i, carry):
        base = pl.multiple_of(i * RG_ROWS, RG_ROWS)
        y = cb + sum(cw[j:j + 1, :] * xe_ref[pl.ds(base + j * w, RG_ROWS), :] for j in range(CONV_W))
        for d, (a, bx) in enumerate(_rg_gates(y, wg, gb, rate)):
            a_ref[d, pl.ds(base, RG_ROWS), :] = a
            bx_ref[d, pl.ds(base, RG_ROWS), :] = bx
        return carry

    lax.fori_loop(0, seq // RG_ROWS, gate_chunk, 0)

    for d in range(2):
        def slab(i):
            r = i if d == 0 else n_rows - 1 - i
            return pl.ds(pl.multiple_of(r * w, w), w)

        def column_totals(i, carry):
            a_tot, h_end = carry
            a = a_ref[d, slab(i), :]
            return a * a_tot, a * h_end + bx_ref[d, slab(i), :]

        a_tot, h_end = lax.fori_loop(0, n_rows, column_totals,
                                     (jnp.ones((w, c), F32), jnp.zeros((w, c), F32)))
        a_cum, h_cum = _scan_rows(a_tot, h_end, reverse=(d == 1))
        after = a_cum * h0[d] + h_cum
        w_idx = lax.broadcasted_iota(jnp.int32, (w, 1), 0)
        if d == 0:
            h_in = jnp.where(w_idx == 0, h0[d], pltpu.roll(after, 1, 0))
        else:
            h_in = jnp.where(w_idx == w - 1, h0[d], pltpu.roll(after, w - 1, 0))

        def emit(i, h):
            h = a_ref[d, slab(i), :] * h + bx_ref[d, slab(i), :]
            if d == 0:
                hf_ref[slab(i), :] = h
            else:
                gate = _gelu_tanh(gl_ref[slab(i), :].astype(F32))
                ol_ref[slab(i), :] = ((hf_ref[slab(i), :] + h) * gate).astype(BF16)
            return h

        lax.fori_loop(0, n_rows, emit, h_in)


def _rglru(p, conv_w, conv_b, wg, gb, lam, n_batch, seq, ctx_len, col0):
    cw_total = conv_w.shape[1]
    n_slabs = cw_total // RG_SLAB
    n_rows = seq // GRID_W
    ctx_base = n_batch * seq // ctx_len
    lat = lambda colblk: pl.BlockSpec((seq, RG_SLAB), lambda b, j: (b, colblk * n_slabs + j))
    ctx = lambda colblk: pl.BlockSpec((ctx_len, RG_SLAB), lambda b, j: (ctx_base + b, colblk * n_slabs + j))
    return pl.pallas_call(
        functools.partial(_rglru_kernel, n_rows=n_rows),
        grid=(n_batch, n_slabs),
        in_specs=[
            lat(col0), ctx(col0), lat(col0 + 1), ctx(col0 + 1),
            pl.BlockSpec((CONV_W, RG_SLAB), lambda b, j: (0, j)),
            pl.BlockSpec((1, RG_SLAB), lambda b, j: (0, j)),
            pl.BlockSpec((None, RG_SLAB, 4 * RG_SLAB), lambda b, j: (j, 0, 0)),
            pl.BlockSpec((None, 1, 4 * RG_SLAB), lambda b, j: (j, 0, 0)),
            pl.BlockSpec((2, RG_SLAB), lambda b, j: (0, j)),
        ],
        out_specs=[pl.BlockSpec((seq, RG_SLAB), lambda b, j: (b, j)),
                   pl.BlockSpec((ctx_len, RG_SLAB), lambda b, j: (b, j))],
        out_shape=[jax.ShapeDtypeStruct((n_batch * seq, cw_total), BF16),
                   jax.ShapeDtypeStruct((n_batch * ctx_len, cw_total), BF16)],
        scratch_shapes=[pltpu.VMEM((seq + 3 * GRID_W, RG_SLAB), F32),
                        pltpu.VMEM((2, seq, RG_SLAB), F32), pltpu.VMEM((2, seq, RG_SLAB), F32),
                        pltpu.VMEM((seq, RG_SLAB), F32)],
        compiler_params=_cparams(("parallel", "arbitrary")),
        name="rglru",
    )(p, p, p, p, conv_w, conv_b, wg, gb, lam)


def _merge_kernel(ya_ref, yb_ref, ycl_ref, ycc_ref, ga_ref, gb_ref, gc_ref, h_ref, mod_ref, bw_ref,
                  wo_ref, o_ref, *, mod_base, n_lat_tiles):
    j = pl.program_id(1)

    @pl.when(j == 0)
    def _():
        o_ref[...] = jnp.zeros_like(o_ref)

    yc = jnp.where(pl.program_id(0) < n_lat_tiles, ycl_ref[...], ycc_ref[...])
    merged = (_sigmoid(ga_ref[...].astype(F32)) * _dot(ya_ref[...], bw_ref[0])
              + _sigmoid(gb_ref[...].astype(F32)) * _dot(yb_ref[...], bw_ref[1])
              + _sigmoid(gc_ref[...].astype(F32)) * _dot(yc, bw_ref[2]))
    o_ref[...] += _dot(merged.astype(BF16), wo_ref[...])

    @pl.when(j == pl.num_programs(1) - 1)
    def _():
        o_ref[...] = h_ref[...] + mod_ref[mod_base:mod_base + 1, :] * o_ref[...]


def _merge(h, n_rows, p, ya, yb, yc_lat, yc_ctx, modtab, mod_row, branch_w, w_out, layer, mod_base):
    d = h.shape[1]
    bw = ya.shape[1]
    n_j = d // MERGE_TILE
    gate0 = 9 * bw // MERGE_TILE
    per_branch = d // MERGE_TILE
    n_lat_tiles = yc_lat.shape[0] // MERGE_ROWS
    branch = pl.BlockSpec((MERGE_ROWS, bw), lambda i, j: (i, 0))
    branch_lat = pl.BlockSpec((MERGE_ROWS, bw), lambda i, j: (jnp.minimum(i, n_lat_tiles - 1), 0))
    branch_ctx = pl.BlockSpec((MERGE_ROWS, bw), lambda i, j: (jnp.maximum(i - n_lat_tiles, 0), 0))
    gate = lambda k: pl.BlockSpec((MERGE_ROWS, MERGE_TILE), lambda i, j: (i, gate0 + k * per_branch + j))
    return pl.pallas_call(
        functools.partial(_merge_kernel, mod_base=mod_base, n_lat_tiles=n_lat_tiles),
        grid=(n_rows // MERGE_ROWS, n_j),
        in_specs=[branch, branch, branch_lat, branch_ctx, gate(0), gate(1), gate(2),
                  pl.BlockSpec((MERGE_ROWS, d), lambda i, j: (i, 0)),
                  pl.BlockSpec((None, N_MOD, d), lambda i, j: (mod_row(MERGE_ROWS)(i), 0, 0)),
                  pl.BlockSpec((None, 3, bw, MERGE_TILE), lambda i, j: (layer, 0, 0, j)),
                  pl.BlockSpec((None, MERGE_TILE, d), lambda i, j: (layer, j, 0))],
        out_specs=pl.BlockSpec((MERGE_ROWS, d), lambda i, j: (i, 0)),
        out_shape=jax.ShapeDtypeStruct((n_rows, d), F32),
        compiler_params=_cparams(("parallel", "arbitrary")),
        name="merge",
    )(ya, yb, yc_lat, yc_ctx, p, p, p, h, modtab, branch_w, w_out)


def kernel(x, c, ctx, c_ctx, mod_w, mod_b, norm_g, ffn_w_in, ffn_w_out, w_in, mlstm_gate_b,
           hgrn_lb_logits, conv_w, conv_b, rg_gate_w, rg_gate_b, rg_lambda, branch_w, w_out, final_g):
    n_batch, seq, d = x.shape
    ctx_len = ctx.shape[1]
    depth = mod_w.shape[0]
    aw = branch_w.shape[2]
    dh = aw // A_HEADS
    n_lat_rows = n_batch * seq
    n_rows = n_lat_rows + n_batch * ctx_len
    assert seq % ROW_TILE == 0 and (n_batch * ctx_len) % ROW_TILE == 0
    assert seq % LA == 0 and ctx_len % LA == 0 and seq % LB == 0 and ctx_len % LB == 0
    assert seq % GRID_W == 0 and seq % RG_ROWS == 0 and n_lat_rows % ctx_len == 0
    assert n_batch + 1 <= MOD_ROWS and aw % RG_SLAB == 0 and RG_SLAB % (aw // C_BLOCKS) == 0

    def mod_row(tile):
        return lambda i: jnp.where(i < n_lat_rows // tile, 1 + i // (seq // tile), 0)

    c16 = jnp.zeros((MOD_ROWS, d), F32).at[0].set(c_ctx).at[1:1 + n_batch].set(c)
    modtab = _mod_table(c16, mod_w, mod_b).reshape(depth, MOD_ROWS, N_MOD, d)
    ffn_w_in_b = ffn_w_in.astype(BF16)
    ffn_w_out_b = ffn_w_out.astype(BF16)
    g0 = 4 * aw
    n_gate = 4 * A_HEADS
    b0 = g0 + n_gate
    w_in_b = w_in.astype(BF16)
    w_main = jnp.concatenate([w_in_b[:, :, :g0], w_in_b[:, :, b0:b0 + aw], w_in_b[:, :, b0 + 3 * aw:],
                              w_in_b[:, :, b0 + aw:b0 + 3 * aw]], axis=-1)
    n16 = (w_main.shape[2] - 2 * aw) // PROJ_TILE
    w_gate = jnp.pad(w_in_b[:, :, g0:g0 + n_gate], ((0, 0), (0, 0), (0, GATE_PAD - n_gate)))
    gate_bias = jnp.pad(mlstm_gate_b.reshape(depth, 1, n_gate), ((0, 0), (0, 0), (0, GATE_PAD - n_gate)))
    lb_p = jax.nn.softmax(hgrn_lb_logits.astype(F32), axis=0)
    lb_all = jnp.cumsum(lb_p, axis=0) - lb_p[0:1]
    branch_w_b = branch_w.astype(BF16)
    w_out_b = w_out.astype(BF16)
    n_slabs = aw // RG_SLAB
    c_db = aw // C_BLOCKS
    per_slab = RG_SLAB // c_db
    blocks = rg_gate_w.reshape(depth, 2, 2, n_slabs, per_slab, c_db, c_db)
    eye = jnp.eye(per_slab, dtype=F32)
    dense = jnp.einsum('ldgspio,pq->ldgspiqo', blocks, eye).reshape(depth, 2, 2, n_slabs, RG_SLAB, RG_SLAB)
    rg_w = dense.transpose(0, 3, 4, 1, 2, 5).reshape(depth, n_slabs, RG_SLAB, 4 * RG_SLAB).astype(BF16)
    rg_b = (rg_gate_b.reshape(depth, 2, 2, n_slabs, RG_SLAB).transpose(0, 3, 1, 2, 4)
            .reshape(depth, n_slabs, 1, 4 * RG_SLAB))
    tri = np.tril(np.ones((LA, LA), np.float32))
    tri3 = [jnp.asarray(np.concatenate([m] * A_PARTS, axis=1), BF16) for m in (tri, tri.T)]
    hgrn_consts = []
    for direction in (0, 1):
        m3, masks = _hgrn_constants(LB, direction)
        hgrn_consts.append((jnp.asarray(m3, BF16), jnp.asarray(masks, F32)))

    h = jnp.concatenate([x.reshape(n_lat_rows, d), ctx.reshape(n_batch * ctx_len, d)], axis=0)
    for layer in range(depth):
        last = layer == depth - 1
        mt = modtab[layer]
        ng = norm_g[layer]
        h = _ffn(h, n_rows, mt, mod_row, ng[0:1], ffn_w_in_b, ffn_w_out_b, layer, 0, 0)
        p, pf, gates = _inproj(h, mt, mod_row, ng[1:2], w_main, w_gate, layer, 3, n16)
        a_fwd = _mlstm(p, gates, gate_bias[layer], tri3[0], n_batch, seq, ctx_len, dh, 0)
        ya = _mlstm(p, gates, gate_bias[layer], tri3[1], n_batch, seq, ctx_len, dh, 1, o_fwd=a_fwd)
        b_fwd = _hgrn(p, pf, lb_all[layer, 0:1], hgrn_consts[0], n_batch, seq, ctx_len, 4, 0)
        yb = _hgrn(p, pf, lb_all[layer, 1:2], hgrn_consts[1], n_batch, seq, ctx_len, 4, 1, o_fwd=b_fwd)
        yc_lat, yc_ctx = _rglru(p, conv_w[layer], conv_b[layer][None, :], rg_w[layer], rg_b[layer],
                                rg_lambda[layer], n_batch, seq, ctx_len, 7)
        rows_out = n_lat_rows if last else n_rows
        h = _merge(h, rows_out, p, ya, yb, yc_lat, yc_ctx, mt, mod_row, branch_w_b, w_out_b, layer, 5)
        h = _ffn(h, rows_out, mt, mod_row, ng[2:3], ffn_w_in_b, ffn_w_out_b, layer, 1, 6,
                 final_g=final_g[None, :] if last else None)
    return h.reshape(n_batch, seq, d)
```

```python
import functools

import numpy as np
import jax
import jax.numpy as jnp
from jax import lax
from jax.experimental import pallas as pl
from jax.experimental.pallas import tpu as pltpu

EPS = 1e-6
N_MOD = 9
GRID_W = 64
A_HEADS = 4
B_DK = 128
C_BLOCKS = 16
CONV_W = 4
RG_C = 8.0
NEG_BIG = -1e30
LOG2E = 1.4426950408889634

MOD_ROWS = 16
GATE_PAD = 128
ROW_TILE = 1024
FF_TILE = 512
FFN_SUB = 512
PROJ_TILE = 1024
MOD_TILE = 1024
MERGE_TILE = 1024
MERGE_ROWS = 512
LA = 256
LB = 128
A_PARTS = 3
B_PARTS = 2
HGRN_GROUP = 8
RG_SLAB = 256
RG_ROWS = 512
RG_UNROLL = 4
VMEM_LIMIT = 60 * 1024 * 1024

F32 = jnp.float32
BF16 = jnp.bfloat16


def _cparams(sem):
    return pltpu.CompilerParams(dimension_semantics=sem, vmem_limit_bytes=VMEM_LIMIT)


def _dot(a, b):
    return jnp.dot(a, b, preferred_element_type=F32)


def _dot_nt(a, b):
    return lax.dot_general(a, b, (((1,), (1,)), ((), ())), preferred_element_type=F32)


def _dot01(m_rep, x):
    parts = m_rep.shape[1] // x.shape[0]
    pieces = []
    rest = x
    for _ in range(parts):
        piece = rest.astype(BF16)
        pieces.append(piece)
        rest = rest - piece.astype(F32)
    return _dot(m_rep, jnp.concatenate(pieces, axis=0))


def _sigmoid(x):
    return 0.5 * jnp.tanh(0.5 * x) + 0.5


def _softplus(z):
    return jnp.maximum(z, 0.0) + jnp.log1p(jnp.exp(-jnp.abs(z)))


def _gelu_tanh(x):
    return 0.5 * x * (1.0 + jnp.tanh(np.sqrt(2.0 / np.pi).astype(np.float32) * (x + 0.044715 * (x * x * x))))


def _rms(x):
    return x * lax.rsqrt(jnp.mean(x * x, axis=-1, keepdims=True) + EPS)


def _adaln(h, g, shift, scale):
    return _rms(h) * (g * (1.0 + scale)) + shift


def _mod_kernel(c_ref, w_ref, b_ref, o_ref):
    cv = c_ref[...]
    sc = (cv * _sigmoid(cv)).astype(BF16)
    o_ref[...] = _dot(sc, w_ref[...].astype(BF16)) + b_ref[...]


def _mod_table(c16, mod_w, mod_b):
    depth, d, n = mod_w.shape
    return pl.pallas_call(
        _mod_kernel,
        grid=(depth, n // MOD_TILE),
        in_specs=[
            pl.BlockSpec((MOD_ROWS, d), lambda l, j: (0, 0)),
            pl.BlockSpec((None, d, MOD_TILE), lambda l, j: (l, 0, j)),
            pl.BlockSpec((None, 1, MOD_TILE), lambda l, j: (l, 0, j)),
        ],
        out_specs=pl.BlockSpec((None, MOD_ROWS, MOD_TILE), lambda l, j: (l, 0, j)),
        out_shape=jax.ShapeDtypeStruct((depth, MOD_ROWS, n), F32),
        compiler_params=_cparams(("arbitrary", "arbitrary")),
        name="mod_table",
    )(c16, mod_w, mod_b.reshape(depth, 1, n))


def _ffn_kernel(h_ref, mod_ref, g_ref, wg_ref, wu_ref, wo_ref, *rest, mod_base, final):
    if final:
        fg_ref, o_ref, xn_ref = rest
    else:
        o_ref, xn_ref = rest
    j = pl.program_id(1)
    subs = [slice(r, r + FFN_SUB) for r in range(0, h_ref.shape[0], FFN_SUB)]

    @pl.when(j == 0)
    def _():
        for rows in subs:
            xn = _adaln(h_ref[rows, :], g_ref[...], mod_ref[mod_base:mod_base + 1, :],
                        mod_ref[mod_base + 1:mod_base + 2, :])
            xn_ref[rows, :] = xn.astype(BF16)
            o_ref[rows, :] = jnp.zeros((FFN_SUB, o_ref.shape[1]), F32)

    for rows in subs:
        xn = xn_ref[rows, :]
        gate = _dot(xn, wg_ref[...])
        up = _dot(xn, wu_ref[...])
        act = (gate * _sigmoid(gate)) * up
        o_ref[rows, :] += _dot(act.astype(BF16), wo_ref[...])

    @pl.when(j == pl.num_programs(1) - 1)
    def _():
        for rows in subs:
            out = h_ref[rows, :] + (0.5 * mod_ref[mod_base + 2:mod_base + 3, :]) * o_ref[rows, :]
            if final:
                out = _rms(out) * fg_ref[...]
            o_ref[rows, :] = out


def _ffn(h, n_rows, modtab, mod_row, g, w_in, w_out, layer, which, mod_base, final_g=None):
    d = h.shape[1]
    d_ff = w_out.shape[2]
    n_ff = d_ff // FF_TILE
    final = final_g is not None
    in_specs = [
        pl.BlockSpec((ROW_TILE, d), lambda i, j: (i, 0)),
        pl.BlockSpec((None, N_MOD, d), lambda i, j: (mod_row(ROW_TILE)(i), 0, 0)),
        pl.BlockSpec((1, d), lambda i, j: (0, 0)),
        pl.BlockSpec((None, None, d, FF_TILE), lambda i, j: (layer, which, 0, j)),
        pl.BlockSpec((None, None, d, FF_TILE), lambda i, j: (layer, which, 0, j + n_ff)),
        pl.BlockSpec((None, None, FF_TILE, d), lambda i, j: (layer, which, j, 0)),
    ]
    args = [h, modtab, g, w_in, w_in, w_out]
    if final:
        in_specs.append(pl.BlockSpec((1, d), lambda i, j: (0, 0)))
        args.append(final_g)
    return pl.pallas_call(
        functools.partial(_ffn_kernel, mod_base=mod_base, final=final),
        grid=(n_rows // ROW_TILE, n_ff),
        in_specs=in_specs,
        out_specs=pl.BlockSpec((ROW_TILE, d), lambda i, j: (i, 0)),
        out_shape=jax.ShapeDtypeStruct((n_rows, d), F32),
        scratch_shapes=[pltpu.VMEM((ROW_TILE, d), BF16)],
        compiler_params=_cparams(("parallel", "arbitrary")),
        name="ffn",
    )(*args)


def _inproj_kernel(h_ref, mod_ref, g_ref, wa_ref, wr_ref, wgate_ref, p16_ref, pf_ref, gate_ref, xn_ref,
                   *, mod_base, n_a, f_lo, n_f):
    j = pl.program_id(1)

    @pl.when(j == 0)
    def _():
        xn = _adaln(h_ref[...], g_ref[...], mod_ref[mod_base:mod_base + 1, :],
                    mod_ref[mod_base + 1:mod_base + 2, :]).astype(BF16)
        xn_ref[...] = xn
        gate_ref[...] = _dot(xn, wgate_ref[...])

    is_pf = jnp.logical_and(j >= f_lo, j < f_lo + n_f)

    @pl.when(j < n_a)
    def _():
        p16_ref[...] = _dot(xn_ref[...], wa_ref[...]).astype(BF16)

    @pl.when(is_pf)
    def _():
        pf_ref[...] = _dot(xn_ref[...], wr_ref[...])

    @pl.when(jnp.logical_and(j >= n_a, jnp.logical_not(is_pf)))
    def _():
        p16_ref[...] = _dot(xn_ref[...], wr_ref[...]).astype(BF16)


def _inproj(h, modtab, mod_row, g, w_head, w_rest, w_gate, layer, mod_base, n_a, f_lo, n_f):
    t, d = h.shape
    n_tiles = n_a + w_rest.shape[2] // PROJ_TILE
    n16 = n_tiles - n_f
    return pl.pallas_call(
        functools.partial(_inproj_kernel, mod_base=mod_base, n_a=n_a, f_lo=f_lo, n_f=n_f),
        grid=(t // ROW_TILE, n_tiles),
        in_specs=[
            pl.BlockSpec((ROW_TILE, d), lambda i, j: (i, 0)),
            pl.BlockSpec((None, N_MOD, d), lambda i, j: (mod_row(ROW_TILE)(i), 0, 0)),
            pl.BlockSpec((1, d), lambda i, j: (0, 0)),
            pl.BlockSpec((None, d, PROJ_TILE), lambda i, j: (layer, 0, jnp.minimum(j, n_a - 1))),
            pl.BlockSpec((None, d, PROJ_TILE), lambda i, j: (layer, 0, jnp.maximum(j - n_a, 0))),
            pl.BlockSpec((None, d, GATE_PAD), lambda i, j: (layer, 0, 0)),
        ],
        out_specs=[
            pl.BlockSpec((ROW_TILE, PROJ_TILE),
                         lambda i, j: (i, jnp.where(j < f_lo, j, jnp.maximum(j - n_f, f_lo - 1)))),
            pl.BlockSpec((ROW_TILE, PROJ_TILE), lambda i, j: (i, jnp.clip(j - f_lo, 0, n_f - 1))),
            pl.BlockSpec((ROW_TILE, GATE_PAD), lambda i, j: (i, 0)),
        ],
        out_shape=[jax.ShapeDtypeStruct((t, n16 * PROJ_TILE), BF16),
                   jax.ShapeDtypeStruct((t, n_f * PROJ_TILE), F32),
                   jax.ShapeDtypeStruct((t, GATE_PAD), F32)],
        scratch_shapes=[pltpu.VMEM((ROW_TILE, d), BF16)],
        compiler_params=_cparams(("parallel", "arbitrary")),
        name="inproj",
    )(h, modtab, g, w_head, w_rest, w_gate)


def _chunk_block(n_batch, seq, ctx_len, length, direction):
    n_lat = seq // length
    n_ctx = ctx_len // length
    ctx_base = n_batch * seq // length

    def block(b, s):
        if direction == 0:
            return jnp.where(s < n_ctx, ctx_base + b * n_ctx + s, b * n_lat + (s - n_ctx))
        return jnp.where(s < n_ctx, ctx_base + b * n_ctx + (n_ctx - 1 - s),
                         b * n_lat + (n_lat - 1 - (s - n_ctx)))

    return block, n_ctx + n_lat


def _mlstm_kernel(q_ref, k_ref, v_ref, g_ref, *rest, dh, direction, final):
    if final:
        of_ref, og_ref, bias_ref, tri_ref, o_ref, c_ref, n_ref, m_ref = rest
    else:
        bias_ref, tri_ref, o_ref, c_ref, n_ref, m_ref = rest
    length = q_ref.shape[0]

    @pl.when(pl.program_id(1) == 0)
    def _():
        c_ref[...] = jnp.zeros_like(c_ref)
        n_ref[...] = jnp.zeros_like(n_ref)
        m_ref[...] = jnp.zeros_like(m_ref)

    row = lax.broadcasted_iota(jnp.int32, (length, length), 0)
    col = lax.broadcasted_iota(jnp.int32, (length, length), 1)
    mask = (row <= col) if direction == 0 else (row >= col)
    last = length - 1 if direction == 0 else 0
    pre = g_ref[...] + bias_ref[...]
    logf = jnp.minimum(pre, 0.0) - jnp.log1p(jnp.exp(-jnp.abs(pre)))
    b_all = _dot01(tri_ref[...], logf)
    b_all_t = b_all.T
    pre_t = pre.T
    for hh in range(A_HEADS):
        ci = direction * 2 * A_HEADS + hh
        cf = ci + A_HEADS
        sl = slice(hh * dh, (hh + 1) * dh)
        b_row = b_all_t[cf:cf + 1, :]
        r_row = pre_t[ci:ci + 1, :] - b_row
        r_col = pre[:, ci:ci + 1] - b_all[:, cf:cf + 1]
        m_prev = m_ref[hh][:, 0:1]
        c_t = c_ref[hh]
        nvec = n_ref[hh]

        dmat = jnp.where(mask, r_col + b_row, NEG_BIG)
        inter = b_row + m_prev
        m_t = jnp.maximum(inter, jnp.max(dmat, axis=0, keepdims=True))
        w_inter = jnp.exp(inter - m_t)
        q = q_ref[:, sl]
        k = (k_ref[:, sl].astype(F32) * (dh ** -0.5)).astype(BF16)
        v_t = v_ref[:, sl].astype(F32).T
        s_t = _dot_nt(k, q) * jnp.exp(dmat - m_t)
        num_t = w_inter * _dot_nt(c_t.astype(BF16), q) + _dot(v_t.astype(BF16), s_t.astype(BF16))
        n_rows = jnp.broadcast_to(nvec, (16, dh)).astype(BF16)
        den = w_inter * _dot_nt(n_rows, q)[0:1, :] + jnp.sum(s_t, axis=0, keepdims=True)
        out = (num_t / jnp.maximum(jnp.abs(den), jnp.exp(-m_t))).T
        if final:
            o_ref[:, sl] = (_rms(out + of_ref[:, sl]) * _sigmoid(og_ref[:, sl].astype(F32))).astype(BF16)
        else:
            o_ref[:, sl] = out

        b_last = b_all_t[cf:cf + 1, last:last + 1]
        wlog = b_last + r_row
        m_new = jnp.maximum(b_last + m_prev, jnp.max(wlog, axis=-1, keepdims=True))
        decay = jnp.exp(b_last + m_prev - m_new)
        ws = jnp.exp(wlog - m_new)
        c_ref[hh] = decay * c_t + _dot((v_t * ws).astype(BF16), k)
        ws_rows = jnp.broadcast_to(ws, (16, length)).astype(BF16)
        n_ref[hh] = decay * nvec + _dot(ws_rows, k)[0:1, :]
        m_ref[hh] = jnp.broadcast_to(m_new, m_ref.shape[1:])


def _mlstm(p, gates, gate_bias, tri3, n_batch, seq, ctx_len, dh, direction, o_fwd=None):
    t = p.shape[0]
    aw = A_HEADS * dh
    block, steps = _chunk_block(n_batch, seq, ctx_len, LA, direction)
    final = o_fwd is not None
    spec = lambda colblk, width: pl.BlockSpec((LA, width), lambda b, s: (block(b, s), colblk))
    const = lambda shape: pl.BlockSpec(shape, lambda b, s: (0,) * len(shape))
    in_specs = [spec(0, aw), spec(1, aw), spec(2, aw), spec(0, GATE_PAD)]
    args = [p, p, p, gates]
    if final:
        in_specs += [spec(0, aw), spec(3, aw)]
        args += [o_fwd, p]
    in_specs += [const((1, GATE_PAD)), const((LA, A_PARTS * LA))]
    args += [gate_bias, tri3]
    return pl.pallas_call(
        functools.partial(_mlstm_kernel, dh=dh, direction=direction, final=final),
        grid=(n_batch, steps),
        in_specs=in_specs,
        out_specs=spec(0, aw),
        out_shape=jax.ShapeDtypeStruct((t, aw), BF16 if final else F32),
        scratch_shapes=[pltpu.VMEM((A_HEADS, dh, dh), F32),
                        pltpu.VMEM((A_HEADS, 1, dh), F32),
                        pltpu.VMEM((A_HEADS, 1, 128), F32)],
        compiler_params=_cparams(("parallel", "arbitrary")),
        name="mlstm_bwd" if final else "mlstm_fwd",
    )(*args)


def _hgrn_levels(length):
    levels = []
    c = length // 2
    while c >= 1:
        levels.append(c)
        c //= 2
    return levels


def _hgrn_constants(length, direction):
    t = np.arange(length)[:, None]
    u = np.arange(length)[None, :]
    if direction == 0:
        mats = [u <= t, u > t]
    else:
        mats = [u >= t, u < t]
    masks = []
    for c in _hgrn_levels(length):
        blk = t // (2 * c)
        pos = t % (2 * c)
        if direction == 0:
            ref = blk * 2 * c + c - 1
            q_role = pos >= c
            m = np.where(q_role, (u > ref) & (u <= t), (u > t) & (u <= ref))
            keep = (blk == blk.T) & q_role & (pos.T < c)
        else:
            ref = blk * 2 * c + c
            q_role = pos < c
            m = np.where(q_role, (u >= t) & (u < ref), (u >= ref) & (u < t))
            keep = (blk == blk.T) & q_role & (pos.T >= c)
        mats.append(m)
        masks.append(keep)
    m_all = np.concatenate(mats, 0).astype(np.float32)
    return np.concatenate([m_all] * B_PARTS, axis=1), np.stack(masks).astype(np.float32)


def _hgrn_kernel(q_ref, f_ref, v_ref, *rest, direction, final):
    if final:
        of_ref, og_ref, lb_ref, m3_ref, mask_ref, o_ref, st_ref, d_ref = rest
    else:
        lb_ref, m3_ref, mask_ref, o_ref, st_ref, d_ref = rest
    length, width = q_ref.shape
    n_heads = width // B_DK
    levels = _hgrn_levels(length)
    last = length - 1 if direction == 0 else 0

    @pl.when(pl.program_id(1) == 0)
    def _():
        st_ref[...] = jnp.zeros_like(st_ref)

    t_idx = lax.broadcasted_iota(jnp.int32, (length, 1), 0)
    eye = (lax.broadcasted_iota(jnp.int32, (length, length), 0)
           == lax.broadcasted_iota(jnp.int32, (length, length), 1))
    gw = HGRN_GROUP * B_DK
    for g in range(n_heads // HGRN_GROUP):
        cols = slice(g * gw, (g + 1) * gw)
        lb = lb_ref[:, cols]
        f = lb + (1.0 - lb) * _sigmoid(f_ref[:, cols])
        kk = 1.0 - f
        qpre = q_ref[:, cols].astype(F32)
        q = qpre * _sigmoid(qpre)
        d_ref[:, cols] = _dot01(m3_ref[...], jnp.log(f) * LOG2E)
        q_in = (q * jnp.exp2(d_ref[0:length, cols])).astype(BF16)
        k_st = (kk * jnp.exp2(d_ref[length:2 * length, cols])).astype(BF16)
        decay = jnp.exp2(d_ref[last:last + 1, cols])
        xs = []
        for li, c in enumerate(levels):
            pos = t_idx % (2 * c)
            q_role = (pos >= c) if direction == 0 else (pos < c)
            e = jnp.exp2(d_ref[(2 + li) * length:(3 + li) * length, cols])
            xs.append((jnp.where(q_role, q, kk) * e).astype(BF16))
        qk = q * kk
        for hg in range(HGRN_GROUP):
            hd = g * HGRN_GROUP + hg
            sl = slice(hd * B_DK, (hd + 1) * B_DK)
            gl = slice(hg * B_DK, (hg + 1) * B_DK)
            att = jnp.where(eye, jnp.sum(qk[:, gl], axis=-1, keepdims=True), 0.0)
            for li in range(len(levels)):
                x = xs[li][:, gl]
                att = att + mask_ref[li] * _dot_nt(x, x)
            vh = v_ref[:, sl]
            st = st_ref[hd]
            out = _dot(att.astype(BF16), vh) + _dot_nt(q_in[:, gl], st.astype(BF16))
            if final:
                og = og_ref[:, sl].astype(F32)
                o_ref[:, sl] = (_rms(out + of_ref[:, sl]) * (og * _sigmoid(og))).astype(BF16)
            else:
                o_ref[:, sl] = out
            st_ref[hd] = st * decay[:, gl] + _dot(vh.astype(F32).T.astype(BF16), k_st[:, gl])


def _hgrn(p, pf, lb, consts, n_batch, seq, ctx_len, col0, direction, o_fwd=None):
    t = p.shape[0]
    bw = lb.shape[1]
    m3, masks = consts
    n_rows = m3.shape[0]
    n_lev = masks.shape[0]
    block, steps = _chunk_block(n_batch, seq, ctx_len, LB, direction)
    final = o_fwd is not None
    spec = lambda colblk: pl.BlockSpec((LB, bw), lambda b, s: (block(b, s), colblk))
    const = lambda shape: pl.BlockSpec(shape, lambda b, s: (0,) * len(shape))
    in_specs = [spec(col0), spec(direction), spec(col0 + 1)]
    args = [p, pf, p]
    if final:
        in_specs += [spec(0), spec(col0 + 2)]
        args += [o_fwd, p]
    in_specs += [const((1, bw)), const((n_rows, B_PARTS * LB)), const((n_lev, LB, LB))]
    args += [lb, m3, masks]
    return pl.pallas_call(
        functools.partial(_hgrn_kernel, direction=direction, final=final),
        grid=(n_batch, steps),
        in_specs=in_specs,
        out_specs=spec(0),
        out_shape=jax.ShapeDtypeStruct((t, bw), BF16 if final else F32),
        scratch_shapes=[pltpu.VMEM((bw // B_DK, B_DK, B_DK), F32),
                        pltpu.VMEM((n_rows, bw), F32)],
        compiler_params=_cparams(("parallel", "arbitrary")),
        name="hgrn_bwd" if final else "hgrn_fwd",
    )(*args)


def _scan_rows(a, h, reverse):
    n = a.shape[0]
    idx = lax.broadcasted_iota(jnp.int32, (n, 1), 0)
    k = 1
    while k < n:
        if reverse:
            a_s, h_s, valid = pltpu.roll(a, n - k, 0), pltpu.roll(h, n - k, 0), idx < n - k
        else:
            a_s, h_s, valid = pltpu.roll(a, k, 0), pltpu.roll(h, k, 0), idx >= k
        h = jnp.where(valid, a * h_s + h, h)
        a = jnp.where(valid, a_s * a, a)
        k *= 2
    return a, h


def _shift_rows(x, delta):
    n = x.shape[0]
    idx = lax.broadcasted_iota(jnp.int32, (n, 1), 0)
    if delta == 0:
        return x
    y = pltpu.roll(x, (-delta) % n, 0)
    valid = (idx + delta >= 0) & (idx + delta < n)
    return jnp.where(valid, y, 0.0)


def _rg_gates(y, wg, gb, rate):
    c = y.shape[1]
    g = _dot(y.astype(BF16), wg) + gb
    out = []
    for d in range(2):
        r = _sigmoid(g[:, 2 * d * c:(2 * d + 1) * c])
        i = _sigmoid(g[:, (2 * d + 1) * c:(2 * d + 2) * c])
        a = jnp.exp2(r * rate[d:d + 1, :])
        out.append((a, jnp.sqrt(1.0 - a * a) * (i * y)))
    return out


def _rglru_kernel(xl_ref, xc_ref, gl_ref, gc_ref, cw_ref, cb_ref, wg_ref, gb_ref, lam_ref,
                  ol_ref, oc_ref, xe_ref, a_ref, bx_ref, hf_ref, *, n_rows):
    w = GRID_W
    seq, c = xl_ref.shape
    rate = (-RG_C * LOG2E) * _softplus(-lam_ref[...])
    cw = cw_ref[...]
    cb = cb_ref[...]
    wg = wg_ref[...]
    gb = gb_ref[...]

    xc = xc_ref[...].astype(F32)
    yc = cb + sum(cw[j:j + 1, :] * _shift_rows(xc, j - 2) for j in range(CONV_W))
    (a_f, bx_f), (a_b, bx_b) = _rg_gates(yc, wg, gb, rate)
    _, hs_f = _scan_rows(a_f, bx_f, reverse=False)
    _, hs_b = _scan_rows(a_b, bx_b, reverse=True)
    n_ctx = xc.shape[0]
    h0 = (hs_f[n_ctx - 1:n_ctx, :], hs_b[0:1, :])
    oc_ref[...] = ((hs_f + hs_b) * _gelu_tanh(gc_ref[...].astype(F32))).astype(BF16)

    xe_ref[2 * w:2 * w + seq, :] = xl_ref[...].astype(F32)
    xe_ref[0:w, :] = _shift_rows(xl_ref[(n_rows - 2) * w:(n_rows - 1) * w, :].astype(F32), -1)
    xe_ref[w:2 * w, :] = _shift_rows(xl_ref[(n_rows - 1) * w:n_rows * w, :].astype(F32), -1)
    xe_ref[2 * w + seq:3 * w + seq, :] = _shift_rows(xl_ref[0:w, :].astype(F32), 1)

    def gate_chunk(i, carry):
        base = pl.multiple_of(i * RG_ROWS, RG_ROWS)
        y = cb + sum(cw[j:j + 1, :] * xe_ref[pl.ds(base + j * w, RG_ROWS), :] for j in range(CONV_W))
        for d, (a, bx) in enumerate(_rg_gates(y, wg, gb, rate)):
            a_ref[d, pl.ds(base, RG_ROWS), :] = a
            bx_ref[d, pl.ds(base, RG_ROWS), :] = bx
        return carry

    lax.fori_loop(0, seq // RG_ROWS, gate_chunk, 0)

    for d in range(2):
        def slab(i):
            r = i if d == 0 else n_rows - 1 - i
            return pl.ds(pl.multiple_of(r * w, w), w)

        def column_totals(i, carry):
            a_tot, h_end = carry
            a = a_ref[d, slab(i), :]
            return a * a_tot, a * h_end + bx_ref[d, slab(i), :]

        a_tot, h_end = lax.fori_loop(0, n_rows, column_totals,
                                     (jnp.ones((w, c), F32), jnp.zeros((w, c), F32)), unroll=RG_UNROLL)
        a_cum, h_cum = _scan_rows(a_tot, h_end, reverse=(d == 1))
        after = a_cum * h0[d] + h_cum
        w_idx = lax.broadcasted_iota(jnp.int32, (w, 1), 0)
        if d == 0:
            h_in = jnp.where(w_idx == 0, h0[d], pltpu.roll(after, 1, 0))
        else:
            h_in = jnp.where(w_idx == w - 1, h0[d], pltpu.roll(after, w - 1, 0))

        def emit(i, h):
            h = a_ref[d, slab(i), :] * h + bx_ref[d, slab(i), :]
            if d == 0:
                hf_ref[slab(i), :] = h
            else:
                gate = _gelu_tanh(gl_ref[slab(i), :].astype(F32))
                ol_ref[slab(i), :] = ((hf_ref[slab(i), :] + h) * gate).astype(BF16)
            return h

        lax.fori_loop(0, n_rows, emit, h_in, unroll=RG_UNROLL)


def _rglru(p, conv_w, conv_b, wg, gb, lam, n_batch, seq, ctx_len, col0):
    cw_total = conv_w.shape[1]
    n_slabs = cw_total // RG_SLAB
    n_rows = seq // GRID_W
    ctx_base = n_batch * seq // ctx_len
    lat = lambda colblk: pl.BlockSpec((seq, RG_SLAB), lambda b, j: (b, colblk * n_slabs + j))
    ctx = lambda colblk: pl.BlockSpec((ctx_len, RG_SLAB), lambda b, j: (ctx_base + b, colblk * n_slabs + j))
    return pl.pallas_call(
        functools.partial(_rglru_kernel, n_rows=n_rows),
        grid=(n_batch, n_slabs),
        in_specs=[
            lat(col0), ctx(col0), lat(col0 + 1), ctx(col0 + 1),
            pl.BlockSpec((CONV_W, RG_SLAB), lambda b, j: (0, j)),
            pl.BlockSpec((1, RG_SLAB), lambda b, j: (0, j)),
            pl.BlockSpec((None, RG_SLAB, 4 * RG_SLAB), lambda b, j: (j, 0, 0)),
            pl.BlockSpec((None, 1, 4 * RG_SLAB), lambda b, j: (j, 0, 0)),
            pl.BlockSpec((2, RG_SLAB), lambda b, j: (0, j)),
        ],
        out_specs=[pl.BlockSpec((seq, RG_SLAB), lambda b, j: (b, j)),
                   pl.BlockSpec((ctx_len, RG_SLAB), lambda b, j: (b, j))],
        out_shape=[jax.ShapeDtypeStruct((n_batch * seq, cw_total), BF16),
                   jax.ShapeDtypeStruct((n_batch * ctx_len, cw_total), BF16)],
        scratch_shapes=[pltpu.VMEM((seq + 3 * GRID_W, RG_SLAB), F32),
                        pltpu.VMEM((2, seq, RG_SLAB), F32), pltpu.VMEM((2, seq, RG_SLAB), F32),
                        pltpu.VMEM((seq, RG_SLAB), F32)],
        compiler_params=_cparams(("parallel", "arbitrary")),
        name="rglru",
    )(p, p, p, p, conv_w, conv_b, wg, gb, lam)


def _merge_kernel(ya_ref, yb_ref, ycl_ref, ycc_ref, ga_ref, gb_ref, gc_ref, h_ref, mod_ref, bw_ref,
                  wo_ref, o_ref, *, mod_base, n_lat_tiles):
    j = pl.program_id(1)

    @pl.when(j == 0)
    def _():
        o_ref[...] = jnp.zeros_like(o_ref)

    yc = jnp.where(pl.program_id(0) < n_lat_tiles, ycl_ref[...], ycc_ref[...])
    merged = (_sigmoid(ga_ref[...].astype(F32)) * _dot(ya_ref[...], bw_ref[0])
              + _sigmoid(gb_ref[...].astype(F32)) * _dot(yb_ref[...], bw_ref[1])
              + _sigmoid(gc_ref[...].astype(F32)) * _dot(yc, bw_ref[2]))
    o_ref[...] += _dot(merged.astype(BF16), wo_ref[...])

    @pl.when(j == pl.num_programs(1) - 1)
    def _():
        o_ref[...] = h_ref[...] + mod_ref[mod_base:mod_base + 1, :] * o_ref[...]


def _merge(h, n_rows, p, ya, yb, yc_lat, yc_ctx, modtab, mod_row, branch_w, w_out, layer, mod_base):
    d = h.shape[1]
    bw = ya.shape[1]
    n_j = d // MERGE_TILE
    gate0 = 9 * bw // MERGE_TILE
    per_branch = d // MERGE_TILE
    n_lat_tiles = yc_lat.shape[0] // MERGE_ROWS
    branch = pl.BlockSpec((MERGE_ROWS, bw), lambda i, j: (i, 0))
    branch_lat = pl.BlockSpec((MERGE_ROWS, bw), lambda i, j: (jnp.minimum(i, n_lat_tiles - 1), 0))
    branch_ctx = pl.BlockSpec((MERGE_ROWS, bw), lambda i, j: (jnp.maximum(i - n_lat_tiles, 0), 0))
    gate = lambda k: pl.BlockSpec((MERGE_ROWS, MERGE_TILE), lambda i, j: (i, gate0 + k * per_branch + j))
    return pl.pallas_call(
        functools.partial(_merge_kernel, mod_base=mod_base, n_lat_tiles=n_lat_tiles),
        grid=(n_rows // MERGE_ROWS, n_j),
        in_specs=[branch, branch, branch_lat, branch_ctx, gate(0), gate(1), gate(2),
                  pl.BlockSpec((MERGE_ROWS, d), lambda i, j: (i, 0)),
                  pl.BlockSpec((None, N_MOD, d), lambda i, j: (mod_row(MERGE_ROWS)(i), 0, 0)),
                  pl.BlockSpec((None, 3, bw, MERGE_TILE), lambda i, j: (layer, 0, 0, j)),
                  pl.BlockSpec((None, MERGE_TILE, d), lambda i, j: (layer, j, 0))],
        out_specs=pl.BlockSpec((MERGE_ROWS, d), lambda i, j: (i, 0)),
        out_shape=jax.ShapeDtypeStruct((n_rows, d), F32),
        compiler_params=_cparams(("parallel", "arbitrary")),
        name="merge",
    )(ya, yb, yc_lat, yc_ctx, p, p, p, h, modtab, branch_w, w_out)


def kernel(x, c, ctx, c_ctx, mod_w, mod_b, norm_g, ffn_w_in, ffn_w_out, w_in, mlstm_gate_b,
           hgrn_lb_logits, conv_w, conv_b, rg_gate_w, rg_gate_b, rg_lambda, branch_w, w_out, final_g):
    n_batch, seq, d = x.shape
    ctx_len = ctx.shape[1]
    depth = mod_w.shape[0]
    aw = branch_w.shape[2]
    dh = aw // A_HEADS
    n_lat_rows = n_batch * seq
    n_rows = n_lat_rows + n_batch * ctx_len
    assert seq % ROW_TILE == 0 and (n_batch * ctx_len) % ROW_TILE == 0
    assert seq % LA == 0 and ctx_len % LA == 0 and seq % LB == 0 and ctx_len % LB == 0
    assert seq % GRID_W == 0 and seq % RG_ROWS == 0 and n_lat_rows % ctx_len == 0
    assert n_batch + 1 <= MOD_ROWS and aw % RG_SLAB == 0 and RG_SLAB % (aw // C_BLOCKS) == 0

    def mod_row(tile):
        return lambda i: jnp.where(i < n_lat_rows // tile, 1 + i // (seq // tile), 0)

    c16 = jnp.zeros((MOD_ROWS, d), F32).at[0].set(c_ctx).at[1:1 + n_batch].set(c)
    modtab = _mod_table(c16, mod_w, mod_b).reshape(depth, MOD_ROWS, N_MOD, d)
    ffn_w_in_b = ffn_w_in.astype(BF16)
    ffn_w_out_b = ffn_w_out.astype(BF16)
    g0 = 4 * aw
    n_gate = 4 * A_HEADS
    b0 = g0 + n_gate
    w_in_b = w_in.astype(BF16)
    w_rest = w_in_b[:, :, b0:]
    n_a = g0 // PROJ_TILE
    f_lo = n_a + aw // PROJ_TILE
    n_f = 2 * aw // PROJ_TILE
    w_gate = jnp.pad(w_in_b[:, :, g0:g0 + n_gate], ((0, 0), (0, 0), (0, GATE_PAD - n_gate)))
    gate_bias = jnp.pad(mlstm_gate_b.reshape(depth, 1, n_gate), ((0, 0), (0, 0), (0, GATE_PAD - n_gate)))
    lb_p = jax.nn.softmax(hgrn_lb_logits.astype(F32), axis=0)
    lb_all = jnp.cumsum(lb_p, axis=0) - lb_p[0:1]
    branch_w_b = branch_w.astype(BF16)
    w_out_b = w_out.astype(BF16)
    n_slabs = aw // RG_SLAB
    c_db = aw // C_BLOCKS
    per_slab = RG_SLAB // c_db
    blocks = rg_gate_w.reshape(depth, 2, 2, n_slabs, per_slab, c_db, c_db)
    eye = jnp.eye(per_slab, dtype=F32)
    dense = jnp.einsum('ldgspio,pq->ldgspiqo', blocks, eye).reshape(depth, 2, 2, n_slabs, RG_SLAB, RG_SLAB)
    rg_w = dense.transpose(0, 3, 4, 1, 2, 5).reshape(depth, n_slabs, RG_SLAB, 4 * RG_SLAB).astype(BF16)
    rg_b = (rg_gate_b.reshape(depth, 2, 2, n_slabs, RG_SLAB).transpose(0, 3, 1, 2, 4)
            .reshape(depth, n_slabs, 1, 4 * RG_SLAB))
    tri = np.tril(np.ones((LA, LA), np.float32))
    tri3 = [jnp.asarray(np.concatenate([m] * A_PARTS, axis=1), BF16) for m in (tri, tri.T)]
    hgrn_consts = []
    for direction in (0, 1):
        m3, masks = _hgrn_constants(LB, direction)
        hgrn_consts.append((jnp.asarray(m3, BF16), jnp.asarray(masks, F32)))

    h = jnp.concatenate([x.reshape(n_lat_rows, d), ctx.reshape(n_batch * ctx_len, d)], axis=0)
    for layer in range(depth):
        last = layer == depth - 1
        mt = modtab[layer]
        ng = norm_g[layer]
        h = _ffn(h, n_rows, mt, mod_row, ng[0:1], ffn_w_in_b, ffn_w_out_b, layer, 0, 0)
        p, pf, gates = _inproj(h, mt, mod_row, ng[1:2], w_in_b, w_rest, w_gate, layer, 3, n_a, f_lo, n_f)
        a_fwd = _mlstm(p, gates, gate_bias[layer], tri3[0], n_batch, seq, ctx_len, dh, 0)
        ya = _mlstm(p, gates, gate_bias[layer], tri3[1], n_batch, seq, ctx_len, dh, 1, o_fwd=a_fwd)
        b_fwd = _hgrn(p, pf, lb_all[layer, 0:1], hgrn_consts[0], n_batch, seq, ctx_len, 4, 0)
        yb = _hgrn(p, pf, lb_all[layer, 1:2], hgrn_consts[1], n_batch, seq, ctx_len, 4, 1, o_fwd=b_fwd)
        yc_lat, yc_ctx = _rglru(p, conv_w[layer], conv_b[layer][None, :], rg_w[layer], rg_b[layer],
                                rg_lambda[layer], n_batch, seq, ctx_len, 7)
        rows_out = n_lat_rows if last else n_rows
        h = _merge(h, rows_out, p, ya, yb, yc_lat, yc_ctx, mt, mod_row, branch_w_b, w_out_b, layer, 5)
        h = _ffn(h, rows_out, mt, mod_row, ng[2:3], ffn_w_in_b, ffn_w_out_b, layer, 1, 6,
                 final_g=final_g[None, :] if last else None)
    return h.reshape(n_batch, seq, d)
```

```python
import functools

import numpy as np
import jax
import jax.numpy as jnp
from jax import lax
from jax.experimental import pallas as pl
from jax.experimental.pallas import tpu as pltpu

EPS = 1e-6
N_MOD = 9
GRID_W = 64
A_HEADS = 4
B_DK = 128
C_BLOCKS = 16
CONV_W = 4
RG_C = 8.0
NEG_BIG = -1e30
LOG2E = 1.4426950408889634

MOD_ROWS = 16
GATE_PAD = 128
ROW_TILE = 1024
FF_TILE = 512
FFN_SUB = 512
PROJ_TILE = 1024
MOD_TILE = 1024
MERGE_TILE = 1024
MERGE_ROWS = 512
LA = 256
LB = 128
A_PARTS = 3
B_PARTS = 2
HGRN_GROUP = 8
RG_SLAB = 256
RG_ROWS = 512
RG_UNROLL = 4
VMEM_LIMIT = 60 * 1024 * 1024

F32 = jnp.float32
BF16 = jnp.bfloat16


def _cparams(sem):
    return pltpu.CompilerParams(dimension_semantics=sem, vmem_limit_bytes=VMEM_LIMIT)


def _dot(a, b):
    return jnp.dot(a, b, preferred_element_type=F32)


def _dot_nt(a, b):
    return lax.dot_general(a, b, (((1,), (1,)), ((), ())), preferred_element_type=F32)


def _dot01(m_rep, x):
    parts = m_rep.shape[1] // x.shape[0]
    pieces = []
    rest = x
    for _ in range(parts):
        piece = rest.astype(BF16)
        pieces.append(piece)
        rest = rest - piece.astype(F32)
    return _dot(m_rep, jnp.concatenate(pieces, axis=0))


def _sigmoid(x):
    return 0.5 * jnp.tanh(0.5 * x) + 0.5


def _softplus(z):
    return jnp.maximum(z, 0.0) + jnp.log1p(jnp.exp(-jnp.abs(z)))


def _gelu_tanh(x):
    return 0.5 * x * (1.0 + jnp.tanh(np.sqrt(2.0 / np.pi).astype(np.float32) * (x + 0.044715 * (x * x * x))))


def _rms(x):
    return x * lax.rsqrt(jnp.mean(x * x, axis=-1, keepdims=True) + EPS)


def _adaln(h, g, shift, scale):
    return _rms(h) * (g * (1.0 + scale)) + shift


def _mod_kernel(c_ref, w_ref, b_ref, o_ref):
    cv = c_ref[...]
    sc = (cv * _sigmoid(cv)).astype(BF16)
    o_ref[...] = _dot(sc, w_ref[...].astype(BF16)) + b_ref[...]


def _mod_table(c16, mod_w, mod_b):
    depth, d, n = mod_w.shape
    return pl.pallas_call(
        _mod_kernel,
        grid=(depth, n // MOD_TILE),
        in_specs=[
            pl.BlockSpec((MOD_ROWS, d), lambda l, j: (0, 0)),
            pl.BlockSpec((None, d, MOD_TILE), lambda l, j: (l, 0, j)),
            pl.BlockSpec((None, 1, MOD_TILE), lambda l, j: (l, 0, j)),
        ],
        out_specs=pl.BlockSpec((None, MOD_ROWS, MOD_TILE), lambda l, j: (l, 0, j)),
        out_shape=jax.ShapeDtypeStruct((depth, MOD_ROWS, n), F32),
        compiler_params=_cparams(("arbitrary", "arbitrary")),
        name="mod_table",
    )(c16, mod_w, mod_b.reshape(depth, 1, n))


def _ffn_kernel(h_ref, mod_ref, g_ref, wg_ref, wu_ref, wo_ref, *rest, mod_base, final):
    if final:
        fg_ref, o_ref, xn_ref = rest
    else:
        o_ref, xn_ref = rest
    j = pl.program_id(1)
    subs = [slice(r, r + FFN_SUB) for r in range(0, h_ref.shape[0], FFN_SUB)]

    @pl.when(j == 0)
    def _():
        for rows in subs:
            xn = _adaln(h_ref[rows, :], g_ref[...], mod_ref[mod_base:mod_base + 1, :],
                        mod_ref[mod_base + 1:mod_base + 2, :])
            xn_ref[rows, :] = xn.astype(BF16)
            o_ref[rows, :] = jnp.zeros((FFN_SUB, o_ref.shape[1]), F32)

    for rows in subs:
        xn = xn_ref[rows, :]
        gate = _dot(xn, wg_ref[...])
        up = _dot(xn, wu_ref[...])
        act = (gate * _sigmoid(gate)) * up
        o_ref[rows, :] += _dot(act.astype(BF16), wo_ref[...])

    @pl.when(j == pl.num_programs(1) - 1)
    def _():
        for rows in subs:
            out = h_ref[rows, :] + (0.5 * mod_ref[mod_base + 2:mod_base + 3, :]) * o_ref[rows, :]
            if final:
                out = _rms(out) * fg_ref[...]
            o_ref[rows, :] = out


def _ffn(h, n_rows, modtab, mod_row, g, w_in, w_out, layer, which, mod_base, final_g=None):
    d = h.shape[1]
    d_ff = w_out.shape[2]
    n_ff = d_ff // FF_TILE
    final = final_g is not None
    in_specs = [
        pl.BlockSpec((ROW_TILE, d), lambda i, j: (i, 0)),
        pl.BlockSpec((None, N_MOD, d), lambda i, j: (mod_row(ROW_TILE)(i), 0, 0)),
        pl.BlockSpec((1, d), lambda i, j: (0, 0)),
        pl.BlockSpec((None, None, d, FF_TILE), lambda i, j: (layer, which, 0, j)),
        pl.BlockSpec((None, None, d, FF_TILE), lambda i, j: (layer, which, 0, j + n_ff)),
        pl.BlockSpec((None, None, FF_TILE, d), lambda i, j: (layer, which, j, 0)),
    ]
    args = [h, modtab, g, w_in, w_in, w_out]
    if final:
        in_specs.append(pl.BlockSpec((1, d), lambda i, j: (0, 0)))
        args.append(final_g)
    return pl.pallas_call(
        functools.partial(_ffn_kernel, mod_base=mod_base, final=final),
        grid=(n_rows // ROW_TILE, n_ff),
        in_specs=in_specs,
        out_specs=pl.BlockSpec((ROW_TILE, d), lambda i, j: (i, 0)),
        out_shape=jax.ShapeDtypeStruct((n_rows, d), F32),
        scratch_shapes=[pltpu.VMEM((ROW_TILE, d), BF16)],
        compiler_params=_cparams(("parallel", "arbitrary")),
        name="ffn",
    )(*args)


def _inproj_kernel(h_ref, mod_ref, g_ref, w_ref, wgate_ref, p16_ref, pf_ref, gate_ref, xn_ref,
                   *, mod_base, n16):
    j = pl.program_id(1)

    @pl.when(j == 0)
    def _():
        xn = _adaln(h_ref[...], g_ref[...], mod_ref[mod_base:mod_base + 1, :],
                    mod_ref[mod_base + 1:mod_base + 2, :]).astype(BF16)
        xn_ref[...] = xn
        gate_ref[...] = _dot(xn, wgate_ref[...])

    @pl.when(j < n16)
    def _():
        p16_ref[...] = _dot(xn_ref[...], w_ref[...]).astype(BF16)

    @pl.when(j >= n16)
    def _():
        pf_ref[...] = _dot(xn_ref[...], w_ref[...])


def _inproj(h, modtab, mod_row, g, w_main, w_gate, layer, mod_base, n16):
    t, d = h.shape
    n = w_main.shape[2]
    n_tiles = n // PROJ_TILE
    return pl.pallas_call(
        functools.partial(_inproj_kernel, mod_base=mod_base, n16=n16),
        grid=(t // ROW_TILE, n_tiles),
        in_specs=[
            pl.BlockSpec((ROW_TILE, d), lambda i, j: (i, 0)),
            pl.BlockSpec((None, N_MOD, d), lambda i, j: (mod_row(ROW_TILE)(i), 0, 0)),
            pl.BlockSpec((1, d), lambda i, j: (0, 0)),
            pl.BlockSpec((None, d, PROJ_TILE), lambda i, j: (layer, 0, j)),
            pl.BlockSpec((None, d, GATE_PAD), lambda i, j: (layer, 0, 0)),
        ],
        out_specs=[
            pl.BlockSpec((ROW_TILE, PROJ_TILE), lambda i, j: (i, jnp.minimum(j, n16 - 1))),
            pl.BlockSpec((ROW_TILE, PROJ_TILE), lambda i, j: (i, jnp.maximum(j - n16, 0))),
            pl.BlockSpec((ROW_TILE, GATE_PAD), lambda i, j: (i, 0)),
        ],
        out_shape=[jax.ShapeDtypeStruct((t, n16 * PROJ_TILE), BF16),
                   jax.ShapeDtypeStruct((t, (n_tiles - n16) * PROJ_TILE), F32),
                   jax.ShapeDtypeStruct((t, GATE_PAD), F32)],
        scratch_shapes=[pltpu.VMEM((ROW_TILE, d), BF16)],
        compiler_params=_cparams(("parallel", "arbitrary")),
        name="inproj",
    )(h, modtab, g, w_main, w_gate)


def _chunk_block(n_batch, seq, ctx_len, length, direction):
    n_lat = seq // length
    n_ctx = ctx_len // length
    ctx_base = n_batch * seq // length

    def block(b, s):
        if direction == 0:
            return jnp.where(s < n_ctx, ctx_base + b * n_ctx + s, b * n_lat + (s - n_ctx))
        return jnp.where(s < n_ctx, ctx_base + b * n_ctx + (n_ctx - 1 - s),
                         b * n_lat + (n_lat - 1 - (s - n_ctx)))

    return block, n_ctx + n_lat


def _mlstm_kernel(q_ref, k_ref, v_ref, g_ref, *rest, dh, direction, final):
    if final:
        of_ref, og_ref, bias_ref, tri_ref, o_ref, c_ref, n_ref, m_ref = rest
    else:
        bias_ref, tri_ref, o_ref, c_ref, n_ref, m_ref = rest
    length = q_ref.shape[0]

    @pl.when(pl.program_id(1) == 0)
    def _():
        c_ref[...] = jnp.zeros_like(c_ref)
        n_ref[...] = jnp.zeros_like(n_ref)
        m_ref[...] = jnp.zeros_like(m_ref)

    row = lax.broadcasted_iota(jnp.int32, (length, length), 0)
    col = lax.broadcasted_iota(jnp.int32, (length, length), 1)
    mask = (row <= col) if direction == 0 else (row >= col)
    last = length - 1 if direction == 0 else 0
    pre = g_ref[...] + bias_ref[...]
    logf = jnp.minimum(pre, 0.0) - jnp.log1p(jnp.exp(-jnp.abs(pre)))
    b_all = _dot01(tri_ref[...], logf)
    b_all_t = b_all.T
    pre_t = pre.T
    for hh in range(A_HEADS):
        ci = direction * 2 * A_HEADS + hh
        cf = ci + A_HEADS
        sl = slice(hh * dh, (hh + 1) * dh)
        b_row = b_all_t[cf:cf + 1, :]
        r_row = pre_t[ci:ci + 1, :] - b_row
        r_col = pre[:, ci:ci + 1] - b_all[:, cf:cf + 1]
        m_prev = m_ref[hh][:, 0:1]
        c_t = c_ref[hh]
        nvec = n_ref[hh]

        dmat = jnp.where(mask, r_col + b_row, NEG_BIG)
        inter = b_row + m_prev
        m_t = jnp.maximum(inter, jnp.max(dmat, axis=0, keepdims=True))
        w_inter = jnp.exp(inter - m_t)
        q = q_ref[:, sl]
        k = (k_ref[:, sl].astype(F32) * (dh ** -0.5)).astype(BF16)
        v_t = v_ref[:, sl].astype(F32).T
        s_t = _dot_nt(k, q) * jnp.exp(dmat - m_t)
        num_t = w_inter * _dot_nt(c_t.astype(BF16), q) + _dot(v_t.astype(BF16), s_t.astype(BF16))
        n_rows = jnp.broadcast_to(nvec, (16, dh)).astype(BF16)
        den = w_inter * _dot_nt(n_rows, q)[0:1, :] + jnp.sum(s_t, axis=0, keepdims=True)
        out = (num_t / jnp.maximum(jnp.abs(den), jnp.exp(-m_t))).T
        if final:
            o_ref[:, sl] = (_rms(out + of_ref[:, sl]) * _sigmoid(og_ref[:, sl].astype(F32))).astype(BF16)
        else:
            o_ref[:, sl] = out

        b_last = b_all_t[cf:cf + 1, last:last + 1]
        wlog = b_last + r_row
        m_new = jnp.maximum(b_last + m_prev, jnp.max(wlog, axis=-1, keepdims=True))
        decay = jnp.exp(b_last + m_prev - m_new)
        ws = jnp.exp(wlog - m_new)
        c_ref[hh] = decay * c_t + _dot((v_t * ws).astype(BF16), k)
        ws_rows = jnp.broadcast_to(ws, (16, length)).astype(BF16)
        n_ref[hh] = decay * nvec + _dot(ws_rows, k)[0:1, :]
        m_ref[hh] = jnp.broadcast_to(m_new, m_ref.shape[1:])


def _mlstm(p, gates, gate_bias, tri3, n_batch, seq, ctx_len, dh, direction, o_fwd=None):
    t = p.shape[0]
    aw = A_HEADS * dh
    block, steps = _chunk_block(n_batch, seq, ctx_len, LA, direction)
    final = o_fwd is not None
    spec = lambda colblk, width: pl.BlockSpec((LA, width), lambda b, s: (block(b, s), colblk))
    const = lambda shape: pl.BlockSpec(shape, lambda b, s: (0,) * len(shape))
    in_specs = [spec(0, aw), spec(1, aw), spec(2, aw), spec(0, GATE_PAD)]
    args = [p, p, p, gates]
    if final:
        in_specs += [spec(0, aw), spec(3, aw)]
        args += [o_fwd, p]
    in_specs += [const((1, GATE_PAD)), const((LA, A_PARTS * LA))]
    args += [gate_bias, tri3]
    return pl.pallas_call(
        functools.partial(_mlstm_kernel, dh=dh, direction=direction, final=final),
        grid=(n_batch, steps),
        in_specs=in_specs,
        out_specs=spec(0, aw),
        out_shape=jax.ShapeDtypeStruct((t, aw), BF16 if final else F32),
        scratch_shapes=[pltpu.VMEM((A_HEADS, dh, dh), F32),
                        pltpu.VMEM((A_HEADS, 1, dh), F32),
                        pltpu.VMEM((A_HEADS, 1, 128), F32)],
        compiler_params=_cparams(("parallel", "arbitrary")),
        name="mlstm_bwd" if final else "mlstm_fwd",
    )(*args)


def _hgrn_levels(length):
    levels = []
    c = length // 2
    while c >= 1:
        levels.append(c)
        c //= 2
    return levels


def _hgrn_constants(length, direction):
    t = np.arange(length)[:, None]
    u = np.arange(length)[None, :]
    if direction == 0:
        mats = [u <= t, u > t]
    else:
        mats = [u >= t, u < t]
    masks = []
    for c in _hgrn_levels(length):
        blk = t // (2 * c)
        pos = t % (2 * c)
        if direction == 0:
            ref = blk * 2 * c + c - 1
            q_role = pos >= c
            m = np.where(q_role, (u > ref) & (u <= t), (u > t) & (u <= ref))
            keep = (blk == blk.T) & q_role & (pos.T < c)
        else:
            ref = blk * 2 * c + c
            q_role = pos < c
            m = np.where(q_role, (u >= t) & (u < ref), (u >= ref) & (u < t))
            keep = (blk == blk.T) & q_role & (pos.T >= c)
        mats.append(m)
        masks.append(keep)
    m_all = np.concatenate(mats, 0).astype(np.float32)
    return np.concatenate([m_all] * B_PARTS, axis=1), np.stack(masks).astype(np.float32)


def _hgrn_kernel(q_ref, f_ref, v_ref, *rest, direction, final):
    if final:
        of_ref, og_ref, lb_ref, m3_ref, mask_ref, o_ref, st_ref, d_ref = rest
    else:
        lb_ref, m3_ref, mask_ref, o_ref, st_ref, d_ref = rest
    length, width = q_ref.shape
    n_heads = width // B_DK
    levels = _hgrn_levels(length)
    last = length - 1 if direction == 0 else 0

    @pl.when(pl.program_id(1) == 0)
    def _():
        st_ref[...] = jnp.zeros_like(st_ref)

    t_idx = lax.broadcasted_iota(jnp.int32, (length, 1), 0)
    eye = (lax.broadcasted_iota(jnp.int32, (length, length), 0)
           == lax.broadcasted_iota(jnp.int32, (length, length), 1))
    gw = HGRN_GROUP * B_DK
    for g in range(n_heads // HGRN_GROUP):
        cols = slice(g * gw, (g + 1) * gw)
        lb = lb_ref[:, cols]
        f = lb + (1.0 - lb) * _sigmoid(f_ref[:, cols])
        kk = 1.0 - f
        qpre = q_ref[:, cols].astype(F32)
        q = qpre * _sigmoid(qpre)
        d_ref[:, cols] = _dot01(m3_ref[...], jnp.log(f) * LOG2E)
        q_in = (q * jnp.exp2(d_ref[0:length, cols])).astype(BF16)
        k_st = (kk * jnp.exp2(d_ref[length:2 * length, cols])).astype(BF16)
        decay = jnp.exp2(d_ref[last:last + 1, cols])
        xs = []
        for li, c in enumerate(levels):
            pos = t_idx % (2 * c)
            q_role = (pos >= c) if direction == 0 else (pos < c)
            e = jnp.exp2(d_ref[(2 + li) * length:(3 + li) * length, cols])
            xs.append((jnp.where(q_role, q, kk) * e).astype(BF16))
        qk = q * kk
        for hg in range(HGRN_GROUP):
            hd = g * HGRN_GROUP + hg
            sl = slice(hd * B_DK, (hd + 1) * B_DK)
            gl = slice(hg * B_DK, (hg + 1) * B_DK)
            att = jnp.where(eye, jnp.sum(qk[:, gl], axis=-1, keepdims=True), 0.0)
            for li in range(len(levels)):
                x = xs[li][:, gl]
                att = att + mask_ref[li] * _dot_nt(x, x)
            vh = v_ref[:, sl]
            st = st_ref[hd]
            out = _dot(att.astype(BF16), vh) + _dot_nt(q_in[:, gl], st.astype(BF16))
            if final:
                og = og_ref[:, sl].astype(F32)
                o_ref[:, sl] = (_rms(out + of_ref[:, sl]) * (og * _sigmoid(og))).astype(BF16)
            else:
                o_ref[:, sl] = out
            st_ref[hd] = st * decay[:, gl] + _dot(vh.astype(F32).T.astype(BF16), k_st[:, gl])


def _hgrn(p, pf, lb, consts, n_batch, seq, ctx_len, col0, direction, o_fwd=None):
    t = p.shape[0]
    bw = lb.shape[1]
    m3, masks = consts
    n_rows = m3.shape[0]
    n_lev = masks.shape[0]
    block, steps = _chunk_block(n_batch, seq, ctx_len, LB, direction)
    final = o_fwd is not None
    spec = lambda colblk: pl.BlockSpec((LB, bw), lambda b, s: (block(b, s), colblk))
    const = lambda shape: pl.BlockSpec(shape, lambda b, s: (0,) * len(shape))
    in_specs = [spec(col0), spec(direction), spec(col0 + 1)]
    args = [p, pf, p]
    if final:
        in_specs += [spec(0), spec(col0 + 2)]
        args += [o_fwd, p]
    in_specs += [const((1, bw)), const((n_rows, B_PARTS * LB)), const((n_lev, LB, LB))]
    args += [lb, m3, masks]
    return pl.pallas_call(
        functools.partial(_hgrn_kernel, direction=direction, final=final),
        grid=(n_batch, steps),
        in_specs=in_specs,
        out_specs=spec(0),
        out_shape=jax.ShapeDtypeStruct((t, bw), BF16 if final else F32),
        scratch_shapes=[pltpu.VMEM((bw // B_DK, B_DK, B_DK), F32),
                        pltpu.VMEM((n_rows, bw), F32)],
        compiler_params=_cparams(("parallel", "arbitrary")),
        name="hgrn_bwd" if final else "hgrn_fwd",
    )(*args)


def _scan_rows(a, h, reverse):
    n = a.shape[0]
    idx = lax.broadcasted_iota(jnp.int32, (n, 1), 0)
    k = 1
    while k < n:
        if reverse:
            a_s, h_s, valid = pltpu.roll(a, n - k, 0), pltpu.roll(h, n - k, 0), idx < n - k
        else:
            a_s, h_s, valid = pltpu.roll(a, k, 0), pltpu.roll(h, k, 0), idx >= k
        h = jnp.where(valid, a * h_s + h, h)
        a = jnp.where(valid, a_s * a, a)
        k *= 2
    return a, h


def _shift_rows(x, delta):
    n = x.shape[0]
    idx = lax.broadcasted_iota(jnp.int32, (n, 1), 0)
    if delta == 0:
        return x
    y = pltpu.roll(x, (-delta) % n, 0)
    valid = (idx + delta >= 0) & (idx + delta < n)
    return jnp.where(valid, y, 0.0)


def _rg_gates(y, wg, gb, rate):
    c = y.shape[1]
    g = _dot(y.astype(BF16), wg) + gb
    out = []
    for d in range(2):
        r = _sigmoid(g[:, 2 * d * c:(2 * d + 1) * c])
        i = _sigmoid(g[:, (2 * d + 1) * c:(2 * d + 2) * c])
        a = jnp.exp2(r * rate[d:d + 1, :])
        out.append((a, jnp.sqrt(1.0 - a * a) * (i * y)))
    return out


def _rglru_kernel(xl_ref, xc_ref, gl_ref, gc_ref, cw_ref, cb_ref, wg_ref, gb_ref, lam_ref,
                  ol_ref, oc_ref, xe_ref, a_ref, bx_ref, hf_ref, *, n_rows):
    w = GRID_W
    seq, c = xl_ref.shape
    rate = (-RG_C * LOG2E) * _softplus(-lam_ref[...])
    cw = cw_ref[...]
    cb = cb_ref[...]
    wg = wg_ref[...]
    gb = gb_ref[...]

    xc = xc_ref[...].astype(F32)
    yc = cb + sum(cw[j:j + 1, :] * _shift_rows(xc, j - 2) for j in range(CONV_W))
    (a_f, bx_f), (a_b, bx_b) = _rg_gates(yc, wg, gb, rate)
    _, hs_f = _scan_rows(a_f, bx_f, reverse=False)
    _, hs_b = _scan_rows(a_b, bx_b, reverse=True)
    n_ctx = xc.shape[0]
    h0 = (hs_f[n_ctx - 1:n_ctx, :], hs_b[0:1, :])
    oc_ref[...] = ((hs_f + hs_b) * _gelu_tanh(gc_ref[...].astype(F32))).astype(BF16)

    xe_ref[2 * w:2 * w + seq, :] = xl_ref[...].astype(F32)
    xe_ref[0:w, :] = _shift_rows(xl_ref[(n_rows - 2) * w:(n_rows - 1) * w, :].astype(F32), -1)
    xe_ref[w:2 * w, :] = _shift_rows(xl_ref[(n_rows - 1) * w:n_rows * w, :].astype(F32), -1)
    xe_ref[2 * w + seq:3 * w + seq, :] = _shift_rows(xl_ref[0:w, :].astype(F32), 1)

    def gate_chunk(i, carry):
        base = pl.multiple_of(i * RG_ROWS, RG_ROWS)
        y = cb + sum(cw[j:j + 1, :] * xe_ref[pl.ds(base + j * w, RG_ROWS), :] for j in range(CONV_W))
        for d, (a, bx) in enumerate(_rg_gates(y, wg, gb, rate)):
            a_ref[d, pl.ds(base, RG_ROWS), :] = a
            bx_ref[d, pl.ds(base, RG_ROWS), :] = bx
        return carry

    lax.fori_loop(0, seq // RG_ROWS, gate_chunk, 0)

    for d in range(2):
        def slab(i):
            r = i if d == 0 else n_rows - 1 - i
            return pl.ds(pl.multiple_of(r * w, w), w)

        def column_totals(i, carry):
            a_tot, h_end = carry
            a = a_ref[d, slab(i), :]
            return a * a_tot, a * h_end + bx_ref[d, slab(i), :]

        a_tot, h_end = lax.fori_loop(0, n_rows, column_totals,
                                     (jnp.ones((w, c), F32), jnp.zeros((w, c), F32)), unroll=RG_UNROLL)
        a_cum, h_cum = _scan_rows(a_tot, h_end, reverse=(d == 1))
        after = a_cum * h0[d] + h_cum
        w_idx = lax.broadcasted_iota(jnp.int32, (w, 1), 0)
        if d == 0:
            h_in = jnp.where(w_idx == 0, h0[d], pltpu.roll(after, 1, 0))
        else:
            h_in = jnp.where(w_idx == w - 1, h0[d], pltpu.roll(after, w - 1, 0))

        def emit(i, h):
            h = a_ref[d, slab(i), :] * h + bx_ref[d, slab(i), :]
            if d == 0:
                hf_ref[slab(i), :] = h
            else:
                gate = _gelu_tanh(gl_ref[slab(i), :].astype(F32))
                ol_ref[slab(i), :] = ((hf_ref[slab(i), :] + h) * gate).astype(BF16)
            return h

        lax.fori_loop(0, n_rows, emit, h_in, unroll=RG_UNROLL)


def _rglru(p, conv_w, conv_b, wg, gb, lam, n_batch, seq, ctx_len, col0):
    cw_total = conv_w.shape[1]
    n_slabs = cw_total // RG_SLAB
    n_rows = seq // GRID_W
    ctx_base = n_batch * seq // ctx_len
    lat = lambda colblk: pl.BlockSpec((seq, RG_SLAB), lambda b, j: (b, colblk * n_slabs + j))
    ctx = lambda colblk: pl.BlockSpec((ctx_len, RG_SLAB), lambda b, j: (ctx_base + b, colblk * n_slabs + j))
    return pl.pallas_call(
        functools.partial(_rglru_kernel, n_rows=n_rows),
        grid=(n_batch, n_slabs),
        in_specs=[
            lat(col0), ctx(col0), lat(col0 + 1), ctx(col0 + 1),
            pl.BlockSpec((CONV_W, RG_SLAB), lambda b, j: (0, j)),
            pl.BlockSpec((1, RG_SLAB), lambda b, j: (0, j)),
            pl.BlockSpec((None, RG_SLAB, 4 * RG_SLAB), lambda b, j: (j, 0, 0)),
            pl.BlockSpec((None, 1, 4 * RG_SLAB), lambda b, j: (j, 0, 0)),
            pl.BlockSpec((2, RG_SLAB), lambda b, j: (0, j)),
        ],
        out_specs=[pl.BlockSpec((seq, RG_SLAB), lambda b, j: (b, j)),
                   pl.BlockSpec((ctx_len, RG_SLAB), lambda b, j: (b, j))],
        out_shape=[jax.ShapeDtypeStruct((n_batch * seq, cw_total), BF16),
                   jax.ShapeDtypeStruct((n_batch * ctx_len, cw_total), BF16)],
        scratch_shapes=[pltpu.VMEM((seq + 3 * GRID_W, RG_SLAB), F32),
                        pltpu.VMEM((2, seq, RG_SLAB), F32), pltpu.VMEM((2, seq, RG_SLAB), F32),
                        pltpu.VMEM((seq, RG_SLAB), F32)],
        compiler_params=_cparams(("parallel", "arbitrary")),
        name="rglru",
    )(p, p, p, p, conv_w, conv_b, wg, gb, lam)


def _merge_kernel(ya_ref, yb_ref, ycl_ref, ycc_ref, ga_ref, gb_ref, gc_ref, h_ref, mod_ref, bw_ref,
                  wo_ref, o_ref, *, mod_base, n_lat_tiles):
    j = pl.program_id(1)

    @pl.when(j == 0)
    def _():
        o_ref[...] = jnp.zeros_like(o_ref)

    yc = jnp.where(pl.program_id(0) < n_lat_tiles, ycl_ref[...], ycc_ref[...])
    merged = (_sigmoid(ga_ref[...].astype(F32)) * _dot(ya_ref[...], bw_ref[0])
              + _sigmoid(gb_ref[...].astype(F32)) * _dot(yb_ref[...], bw_ref[1])
              + _sigmoid(gc_ref[...].astype(F32)) * _dot(yc, bw_ref[2]))
    o_ref[...] += _dot(merged.astype(BF16), wo_ref[...])

    @pl.when(j == pl.num_programs(1) - 1)
    def _():
        o_ref[...] = h_ref[...] + mod_ref[mod_base:mod_base + 1, :] * o_ref[...]


def _merge(h, n_rows, p, ya, yb, yc_lat, yc_ctx, modtab, mod_row, branch_w, w_out, layer, mod_base):
    d = h.shape[1]
    bw = ya.shape[1]
    n_j = d // MERGE_TILE
    gate0 = 9 * bw // MERGE_TILE
    per_branch = d // MERGE_TILE
    n_lat_tiles = yc_lat.shape[0] // MERGE_ROWS
    branch = pl.BlockSpec((MERGE_ROWS, bw), lambda i, j: (i, 0))
    branch_lat = pl.BlockSpec((MERGE_ROWS, bw), lambda i, j: (jnp.minimum(i, n_lat_tiles - 1), 0))
    branch_ctx = pl.BlockSpec((MERGE_ROWS, bw), lambda i, j: (jnp.maximum(i - n_lat_tiles, 0), 0))
    gate = lambda k: pl.BlockSpec((MERGE_ROWS, MERGE_TILE), lambda i, j: (i, gate0 + k * per_branch + j))
    return pl.pallas_call(
        functools.partial(_merge_kernel, mod_base=mod_base, n_lat_tiles=n_lat_tiles),
        grid=(n_rows // MERGE_ROWS, n_j),
        in_specs=[branch, branch, branch_lat, branch_ctx, gate(0), gate(1), gate(2),
                  pl.BlockSpec((MERGE_ROWS, d), lambda i, j: (i, 0)),
                  pl.BlockSpec((None, N_MOD, d), lambda i, j: (mod_row(MERGE_ROWS)(i), 0, 0)),
                  pl.BlockSpec((None, 3, bw, MERGE_TILE), lambda i, j: (layer, 0, 0, j)),
                  pl.BlockSpec((None, MERGE_TILE, d), lambda i, j: (layer, j, 0))],
        out_specs=pl.BlockSpec((MERGE_ROWS, d), lambda i, j: (i, 0)),
        out_shape=jax.ShapeDtypeStruct((n_rows, d), F32),
        compiler_params=_cparams(("parallel", "arbitrary")),
        name="merge",
    )(ya, yb, yc_lat, yc_ctx, p, p, p, h, modtab, branch_w, w_out)


def kernel(x, c, ctx, c_ctx, mod_w, mod_b, norm_g, ffn_w_in, ffn_w_out, w_in, mlstm_gate_b,
           hgrn_lb_logits, conv_w, conv_b, rg_gate_w, rg_gate_b, rg_lambda, branch_w, w_out, final_g):
    n_batch, seq, d = x.shape
    ctx_len = ctx.shape[1]
    depth = mod_w.shape[0]
    aw = branch_w.shape[2]
    dh = aw // A_HEADS
    n_lat_rows = n_batch * seq
    n_rows = n_lat_rows + n_batch * ctx_len
    assert seq % ROW_TILE == 0 and (n_batch * ctx_len) % ROW_TILE == 0
    assert seq % LA == 0 and ctx_len % LA == 0 and seq % LB == 0 and ctx_len % LB == 0
    assert seq % GRID_W == 0 and seq % RG_ROWS == 0 and n_lat_rows % ctx_len == 0
    assert n_batch + 1 <= MOD_ROWS and aw % RG_SLAB == 0 and RG_SLAB % (aw // C_BLOCKS) == 0

    def mod_row(tile):
        return lambda i: jnp.where(i < n_lat_rows // tile, 1 + i // (seq // tile), 0)

    c16 = jnp.zeros((MOD_ROWS, d), F32).at[0].set(c_ctx).at[1:1 + n_batch].set(c)
    modtab = _mod_table(c16, mod_w, mod_b).reshape(depth, MOD_ROWS, N_MOD, d)
    ffn_w_in_b = ffn_w_in.astype(BF16)
    ffn_w_out_b = ffn_w_out.astype(BF16)
    g0 = 4 * aw
    n_gate = 4 * A_HEADS
    b0 = g0 + n_gate
    w_in_b = w_in.astype(BF16)
    w_main = jnp.concatenate([w_in_b[:, :, :g0], w_in_b[:, :, b0:b0 + aw], w_in_b[:, :, b0 + 3 * aw:],
                              w_in_b[:, :, b0 + aw:b0 + 3 * aw]], axis=-1)
    n16 = (w_main.shape[2] - 2 * aw) // PROJ_TILE
    w_gate = jnp.pad(w_in_b[:, :, g0:g0 + n_gate], ((0, 0), (0, 0), (0, GATE_PAD - n_gate)))
    gate_bias = jnp.pad(mlstm_gate_b.reshape(depth, 1, n_gate), ((0, 0), (0, 0), (0, GATE_PAD - n_gate)))
    lb_p = jax.nn.softmax(hgrn_lb_logits.astype(F32), axis=0)
    lb_all = jnp.cumsum(lb_p, axis=0) - lb_p[0:1]
    branch_w_b = branch_w.astype(BF16)
    w_out_b = w_out.astype(BF16)
    n_slabs = aw // RG_SLAB
    c_db = aw // C_BLOCKS
    per_slab = RG_SLAB // c_db
    blocks = rg_gate_w.reshape(depth, 2, 2, n_slabs, per_slab, c_db, c_db)
    eye = jnp.eye(per_slab, dtype=F32)
    dense = jnp.einsum('ldgspio,pq->ldgspiqo', blocks, eye).reshape(depth, 2, 2, n_slabs, RG_SLAB, RG_SLAB)
    rg_w = dense.transpose(0, 3, 4, 1, 2, 5).reshape(depth, n_slabs, RG_SLAB, 4 * RG_SLAB).astype(BF16)
    rg_b = (rg_gate_b.reshape(depth, 2, 2, n_slabs, RG_SLAB).transpose(0, 3, 1, 2, 4)
            .reshape(depth, n_slabs, 1, 4 * RG_SLAB))
    tri = np.tril(np.ones((LA, LA), np.float32))
    tri3 = [jnp.asarray(np.concatenate([m] * A_PARTS, axis=1), BF16) for m in (tri, tri.T)]
    hgrn_consts = []
    for direction in (0, 1):
        m3, masks = _hgrn_constants(LB, direction)
        hgrn_consts.append((jnp.asarray(m3, BF16), jnp.asarray(masks, F32)))

    h = jnp.concatenate([x.reshape(n_lat_rows, d), ctx.reshape(n_batch * ctx_len, d)], axis=0)
    for layer in range(depth):
        last = layer == depth - 1
        mt = modtab[layer]
        ng = norm_g[layer]
        h = _ffn(h, n_rows, mt, mod_row, ng[0:1], ffn_w_in_b, ffn_w_out_b, layer, 0, 0)
        p, pf, gates = _inproj(h, mt, mod_row, ng[1:2], w_main, w_gate, layer, 3, n16)
        a_fwd = _mlstm(p, gates, gate_bias[layer], tri3[0], n_batch, seq, ctx_len, dh, 0)
        ya = _mlstm(p, gates, gate_bias[layer], tri3[1], n_batch, seq, ctx_len, dh, 1, o_fwd=a_fwd)
        b_fwd = _hgrn(p, pf, lb_all[layer, 0:1], hgrn_consts[0], n_batch, seq, ctx_len, 4, 0)
        yb = _hgrn(p, pf, lb_all[layer, 1:2], hgrn_consts[1], n_batch, seq, ctx_len, 4, 1, o_fwd=b_fwd)
        yc_lat, yc_ctx = _rglru(p, conv_w[layer], conv_b[layer][None, :], rg_w[layer], rg_b[layer],
                                rg_lambda[layer], n_batch, seq, ctx_len, 7)
        rows_out = n_lat_rows if last else n_rows
        h = _merge(h, rows_out, p, ya, yb, yc_lat, yc_ctx, mt, mod_row, branch_w_b, w_out_b, layer, 5)
        h = _ffn(h, rows_out, mt, mod_row, ng[2:3], ffn_w_in_b, ffn_w_out_b, layer, 1, 6,
                 final_g=final_g[None, :] if last else None)
    return h.reshape(n_batch, seq, d)
```

```python
import functools

import numpy as np
import jax
import jax.numpy as jnp
from jax import lax
from jax.experimental import pallas as pl
from jax.experimental.pallas import tpu as pltpu

EPS = 1e-6
N_MOD = 9
GRID_W = 64
A_HEADS = 4
B_DK = 128
C_BLOCKS = 16
CONV_W = 4
RG_C = 8.0
NEG_BIG = -1e30
LOG2E = 1.4426950408889634

MOD_ROWS = 16
GATE_PAD = 128
ROW_TILE = 1024
FF_TILE = 512
FFN_SUB = 512
NORM_STRIP = 16
PROJ_TILE = 1024
MOD_TILE = 1024
MERGE_TILE = 1024
MERGE_ROWS = 512
LA = 256
LB = 128
A_PARTS = 3
B_PARTS = 2
HGRN_GROUP = 8
RG_SLAB = 256
RG_ROWS = 512
RG_UNROLL = 4
VMEM_LIMIT = 60 * 1024 * 1024

F32 = jnp.float32
BF16 = jnp.bfloat16


def _cparams(sem):
    return pltpu.CompilerParams(dimension_semantics=sem, vmem_limit_bytes=VMEM_LIMIT)


def _dot(a, b):
    return jnp.dot(a, b, preferred_element_type=F32)


def _dot_nt(a, b):
    return lax.dot_general(a, b, (((1,), (1,)), ((), ())), preferred_element_type=F32)


def _dot01(m_rep, x):
    parts = m_rep.shape[1] // x.shape[0]
    pieces = []
    rest = x
    for _ in range(parts):
        piece = rest.astype(BF16)
        pieces.append(piece)
        rest = rest - piece.astype(F32)
    return _dot(m_rep, jnp.concatenate(pieces, axis=0))


def _sigmoid(x):
    return 0.5 * jnp.tanh(0.5 * x) + 0.5


def _softplus(z):
    return jnp.maximum(z, 0.0) + jnp.log1p(jnp.exp(-jnp.abs(z)))


def _gelu_tanh(x):
    return 0.5 * x * (1.0 + jnp.tanh(np.sqrt(2.0 / np.pi).astype(np.float32) * (x + 0.044715 * (x * x * x))))


def _rms(x):
    return x * lax.rsqrt(jnp.mean(x * x, axis=-1, keepdims=True) + EPS)


def _mod_kernel(c_ref, w_ref, b_ref, o_ref):
    cv = c_ref[...]
    sc = (cv * _sigmoid(cv)).astype(BF16)
    o_ref[...] = _dot(sc, w_ref[...].astype(BF16)) + b_ref[...]


def _mod_table(c16, mod_w, mod_b):
    depth, d, n = mod_w.shape
    return pl.pallas_call(
        _mod_kernel,
        grid=(depth, n // MOD_TILE),
        in_specs=[
            pl.BlockSpec((MOD_ROWS, d), lambda l, j: (0, 0)),
            pl.BlockSpec((None, d, MOD_TILE), lambda l, j: (l, 0, j)),
            pl.BlockSpec((None, 1, MOD_TILE), lambda l, j: (l, 0, j)),
        ],
        out_specs=pl.BlockSpec((None, MOD_ROWS, MOD_TILE), lambda l, j: (l, 0, j)),
        out_shape=jax.ShapeDtypeStruct((depth, MOD_ROWS, n), F32),
        compiler_params=_cparams(("arbitrary", "arbitrary")),
        name="mod_table",
    )(c16, mod_w, mod_b.reshape(depth, 1, n))


def _ffn_kernel(h_ref, mod_ref, g_ref, wg_ref, wu_ref, wo_ref, *rest, mod_base, final):
    if final:
        fg_ref, o_ref, xn_ref = rest
    else:
        o_ref, xn_ref = rest
    j = pl.program_id(1)
    subs = [slice(r, r + FFN_SUB) for r in range(0, h_ref.shape[0], FFN_SUB)]

    @pl.when(j == 0)
    def _():
        gain = g_ref[...] * (1.0 + mod_ref[mod_base + 1:mod_base + 2, :])
        shift = mod_ref[mod_base:mod_base + 1, :]

        def strip(r, carry):
            rows = pl.ds(pl.multiple_of(r * NORM_STRIP, NORM_STRIP), NORM_STRIP)
            xn_ref[rows, :] = (_rms(h_ref[rows, :]) * gain + shift).astype(BF16)
            o_ref[rows, :] = jnp.zeros((NORM_STRIP, o_ref.shape[1]), F32)
            return carry

        lax.fori_loop(0, h_ref.shape[0] // NORM_STRIP, strip, 0, unroll=8)

    for rows in subs:
        xn = xn_ref[rows, :]
        gate = _dot(xn, wg_ref[...])
        up = _dot(xn, wu_ref[...])
        act = (gate * _sigmoid(gate)) * up
        o_ref[rows, :] += _dot(act.astype(BF16), wo_ref[...])

    @pl.when(j == pl.num_programs(1) - 1)
    def _():
        for rows in subs:
            out = h_ref[rows, :] + (0.5 * mod_ref[mod_base + 2:mod_base + 3, :]) * o_ref[rows, :]
            if final:
                out = _rms(out) * fg_ref[...]
            o_ref[rows, :] = out


def _ffn(h, n_rows, modtab, mod_row, g, w_in, w_out, layer, which, mod_base, final_g=None):
    d = h.shape[1]
    d_ff = w_out.shape[2]
    n_ff = d_ff // FF_TILE
    final = final_g is not None
    in_specs = [
        pl.BlockSpec((ROW_TILE, d), lambda i, j: (i, 0)),
        pl.BlockSpec((None, N_MOD, d), lambda i, j: (mod_row(ROW_TILE)(i), 0, 0)),
        pl.BlockSpec((1, d), lambda i, j: (0, 0)),
        pl.BlockSpec((None, None, d, FF_TILE), lambda i, j: (layer, which, 0, j)),
        pl.BlockSpec((None, None, d, FF_TILE), lambda i, j: (layer, which, 0, j + n_ff)),
        pl.BlockSpec((None, None, FF_TILE, d), lambda i, j: (layer, which, j, 0)),
    ]
    args = [h, modtab, g, w_in, w_in, w_out]
    if final:
        in_specs.append(pl.BlockSpec((1, d), lambda i, j: (0, 0)))
        args.append(final_g)
    return pl.pallas_call(
        functools.partial(_ffn_kernel, mod_base=mod_base, final=final),
        grid=(n_rows // ROW_TILE, n_ff),
        in_specs=in_specs,
        out_specs=pl.BlockSpec((ROW_TILE, d), lambda i, j: (i, 0)),
        out_shape=jax.ShapeDtypeStruct((n_rows, d), F32),
        scratch_shapes=[pltpu.VMEM((ROW_TILE, d), BF16)],
        compiler_params=_cparams(("parallel", "arbitrary")),
        name="ffn",
    )(*args)


def _inproj_kernel(h_ref, mod_ref, g_ref, w_ref, wgate_ref, p16_ref, pf_ref, gate_ref, xn_ref,
                   *, mod_base, n16):
    j = pl.program_id(1)

    @pl.when(j == 0)
    def _():
        gain = g_ref[...] * (1.0 + mod_ref[mod_base + 1:mod_base + 2, :])
        shift = mod_ref[mod_base:mod_base + 1, :]

        def strip(r, carry):
            rows = pl.ds(pl.multiple_of(r * NORM_STRIP, NORM_STRIP), NORM_STRIP)
            xn_ref[rows, :] = (_rms(h_ref[rows, :]) * gain + shift).astype(BF16)
            return carry

        lax.fori_loop(0, h_ref.shape[0] // NORM_STRIP, strip, 0, unroll=8)
        gate_ref[...] = _dot(xn_ref[...], wgate_ref[...])

    @pl.when(j < n16)
    def _():
        p16_ref[...] = _dot(xn_ref[...], w_ref[...]).astype(BF16)

    @pl.when(j >= n16)
    def _():
        pf_ref[...] = _dot(xn_ref[...], w_ref[...])


def _inproj(h, modtab, mod_row, g, w_main, w_gate, layer, mod_base, n16):
    t, d = h.shape
    n = w_main.shape[2]
    n_tiles = n // PROJ_TILE
    return pl.pallas_call(
        functools.partial(_inproj_kernel, mod_base=mod_base, n16=n16),
        grid=(t // ROW_TILE, n_tiles),
        in_specs=[
            pl.BlockSpec((ROW_TILE, d), lambda i, j: (i, 0)),
            pl.BlockSpec((None, N_MOD, d), lambda i, j: (mod_row(ROW_TILE)(i), 0, 0)),
            pl.BlockSpec((1, d), lambda i, j: (0, 0)),
            pl.BlockSpec((None, d, PROJ_TILE), lambda i, j: (layer, 0, j)),
            pl.BlockSpec((None, d, GATE_PAD), lambda i, j: (layer, 0, 0)),
        ],
        out_specs=[
            pl.BlockSpec((ROW_TILE, PROJ_TILE), lambda i, j: (i, jnp.minimum(j, n16 - 1))),
            pl.BlockSpec((ROW_TILE, PROJ_TILE), lambda i, j: (i, jnp.maximum(j - n16, 0))),
            pl.BlockSpec((ROW_TILE, GATE_PAD), lambda i, j: (i, 0)),
        ],
        out_shape=[jax.ShapeDtypeStruct((t, n16 * PROJ_TILE), BF16),
                   jax.ShapeDtypeStruct((t, (n_tiles - n16) * PROJ_TILE), F32),
                   jax.ShapeDtypeStruct((t, GATE_PAD), F32)],
        scratch_shapes=[pltpu.VMEM((ROW_TILE, d), BF16)],
        compiler_params=_cparams(("parallel", "arbitrary")),
        name="inproj",
    )(h, modtab, g, w_main, w_gate)


def _chunk_block(n_batch, seq, ctx_len, length, direction):
    n_lat = seq // length
    n_ctx = ctx_len // length
    ctx_base = n_batch * seq // length

    def block(b, s):
        if direction == 0:
            return jnp.where(s < n_ctx, ctx_base + b * n_ctx + s, b * n_lat + (s - n_ctx))
        return jnp.where(s < n_ctx, ctx_base + b * n_ctx + (n_ctx - 1 - s),
                         b * n_lat + (n_lat - 1 - (s - n_ctx)))

    return block, n_ctx + n_lat


def _mlstm_kernel(q_ref, k_ref, v_ref, g_ref, *rest, dh, direction, final):
    if final:
        of_ref, og_ref, bias_ref, tri_ref, o_ref, c_ref, n_ref, m_ref = rest
    else:
        bias_ref, tri_ref, o_ref, c_ref, n_ref, m_ref = rest
    length = q_ref.shape[0]

    @pl.when(pl.program_id(1) == 0)
    def _():
        c_ref[...] = jnp.zeros_like(c_ref)
        n_ref[...] = jnp.zeros_like(n_ref)
        m_ref[...] = jnp.zeros_like(m_ref)

    row = lax.broadcasted_iota(jnp.int32, (length, length), 0)
    col = lax.broadcasted_iota(jnp.int32, (length, length), 1)
    mask = (row <= col) if direction == 0 else (row >= col)
    last = length - 1 if direction == 0 else 0
    pre = g_ref[...] + bias_ref[...]
    logf = jnp.minimum(pre, 0.0) - jnp.log1p(jnp.exp(-jnp.abs(pre)))
    b_all = _dot01(tri_ref[...], logf)
    b_all_t = b_all.T
    pre_t = pre.T
    for hh in range(A_HEADS):
        ci = direction * 2 * A_HEADS + hh
        cf = ci + A_HEADS
        sl = slice(hh * dh, (hh + 1) * dh)
        b_row = b_all_t[cf:cf + 1, :]
        r_row = pre_t[ci:ci + 1, :] - b_row
        r_col = pre[:, ci:ci + 1] - b_all[:, cf:cf + 1]
        m_prev = m_ref[hh][:, 0:1]
        c_t = c_ref[hh]
        nvec = n_ref[hh]

        dmat = jnp.where(mask, r_col + b_row, NEG_BIG)
        inter = b_row + m_prev
        m_t = jnp.maximum(inter, jnp.max(dmat, axis=0, keepdims=True))
        w_inter = jnp.exp(inter - m_t)
        q = q_ref[:, sl]
        k = (k_ref[:, sl].astype(F32) * (dh ** -0.5)).astype(BF16)
        v_t = v_ref[:, sl].astype(F32).T
        s_t = _dot_nt(k, q) * jnp.exp(dmat - m_t)
        num_t = w_inter * _dot_nt(c_t.astype(BF16), q) + _dot(v_t.astype(BF16), s_t.astype(BF16))
        n_rows = jnp.broadcast_to(nvec, (16, dh)).astype(BF16)
        den = w_inter * _dot_nt(n_rows, q)[0:1, :] + jnp.sum(s_t, axis=0, keepdims=True)
        out = (num_t / jnp.maximum(jnp.abs(den), jnp.exp(-m_t))).T
        if final:
            o_ref[:, sl] = (_rms(out + of_ref[:, sl]) * _sigmoid(og_ref[:, sl].astype(F32))).astype(BF16)
        else:
            o_ref[:, sl] = out

        b_last = b_all_t[cf:cf + 1, last:last + 1]
        wlog = b_last + r_row
        m_new = jnp.maximum(b_last + m_prev, jnp.max(wlog, axis=-1, keepdims=True))
        decay = jnp.exp(b_last + m_prev - m_new)
        ws = jnp.exp(wlog - m_new)
        c_ref[hh] = decay * c_t + _dot((v_t * ws).astype(BF16), k)
        ws_rows = jnp.broadcast_to(ws, (16, length)).astype(BF16)
        n_ref[hh] = decay * nvec + _dot(ws_rows, k)[0:1, :]
        m_ref[hh] = jnp.broadcast_to(m_new, m_ref.shape[1:])


def _mlstm(p, gates, gate_bias, tri3, n_batch, seq, ctx_len, dh, direction, o_fwd=None):
    t = p.shape[0]
    aw = A_HEADS * dh
    block, steps = _chunk_block(n_batch, seq, ctx_len, LA, direction)
    final = o_fwd is not None
    spec = lambda colblk, width: pl.BlockSpec((LA, width), lambda b, s: (block(b, s), colblk))
    const = lambda shape: pl.BlockSpec(shape, lambda b, s: (0,) * len(shape))
    in_specs = [spec(0, aw), spec(1, aw), spec(2, aw), spec(0, GATE_PAD)]
    args = [p, p, p, gates]
    if final:
        in_specs += [spec(0, aw), spec(3, aw)]
        args += [o_fwd, p]
    in_specs += [const((1, GATE_PAD)), const((LA, A_PARTS * LA))]
    args += [gate_bias, tri3]
    return pl.pallas_call(
        functools.partial(_mlstm_kernel, dh=dh, direction=direction, final=final),
        grid=(n_batch, steps),
        in_specs=in_specs,
        out_specs=spec(0, aw),
        out_shape=jax.ShapeDtypeStruct((t, aw), BF16 if final else F32),
        scratch_shapes=[pltpu.VMEM((A_HEADS, dh, dh), F32),
                        pltpu.VMEM((A_HEADS, 1, dh), F32),
                        pltpu.VMEM((A_HEADS, 1, 128), F32)],
        compiler_params=_cparams(("parallel", "arbitrary")),
        name="mlstm_bwd" if final else "mlstm_fwd",
    )(*args)


def _hgrn_levels(length):
    levels = []
    c = length // 2
    while c >= 1:
        levels.append(c)
        c //= 2
    return levels


def _hgrn_constants(length, direction):
    t = np.arange(length)[:, None]
    u = np.arange(length)[None, :]
    if direction == 0:
        mats = [u <= t, u > t]
    else:
        mats = [u >= t, u < t]
    masks = []
    for c in _hgrn_levels(length):
        blk = t // (2 * c)
        pos = t % (2 * c)
        if direction == 0:
            ref = blk * 2 * c + c - 1
            q_role = pos >= c
            m = np.where(q_role, (u > ref) & (u <= t), (u > t) & (u <= ref))
            keep = (blk == blk.T) & q_role & (pos.T < c)
        else:
            ref = blk * 2 * c + c
            q_role = pos < c
            m = np.where(q_role, (u >= t) & (u < ref), (u >= ref) & (u < t))
            keep = (blk == blk.T) & q_role & (pos.T >= c)
        mats.append(m)
        masks.append(keep)
    m_all = np.concatenate(mats, 0).astype(np.float32)
    return np.concatenate([m_all] * B_PARTS, axis=1), np.stack(masks).astype(np.float32)


def _hgrn_kernel(q_ref, f_ref, v_ref, *rest, direction, final):
    if final:
        of_ref, og_ref, lb_ref, m3_ref, mask_ref, o_ref, st_ref, d_ref = rest
    else:
        lb_ref, m3_ref, mask_ref, o_ref, st_ref, d_ref = rest
    length, width = q_ref.shape
    n_heads = width // B_DK
    levels = _hgrn_levels(length)
    last = length - 1 if direction == 0 else 0

    @pl.when(pl.program_id(1) == 0)
    def _():
        st_ref[...] = jnp.zeros_like(st_ref)

    t_idx = lax.broadcasted_iota(jnp.int32, (length, 1), 0)
    eye = (lax.broadcasted_iota(jnp.int32, (length, length), 0)
           == lax.broadcasted_iota(jnp.int32, (length, length), 1))
    gw = HGRN_GROUP * B_DK
    for g in range(n_heads // HGRN_GROUP):
        cols = slice(g * gw, (g + 1) * gw)
        lb = lb_ref[:, cols]
        f = lb + (1.0 - lb) * _sigmoid(f_ref[:, cols])
        kk = 1.0 - f
        qpre = q_ref[:, cols].astype(F32)
        q = qpre * _sigmoid(qpre)
        d_ref[:, cols] = _dot01(m3_ref[...], jnp.log(f) * LOG2E)
        q_in = (q * jnp.exp2(d_ref[0:length, cols])).astype(BF16)
        k_st = (kk * jnp.exp2(d_ref[length:2 * length, cols])).astype(BF16)
        decay = jnp.exp2(d_ref[last:last + 1, cols])
        xs = []
        for li, c in enumerate(levels):
            pos = t_idx % (2 * c)
            q_role = (pos >= c) if direction == 0 else (pos < c)
            e = jnp.exp2(d_ref[(2 + li) * length:(3 + li) * length, cols])
            xs.append((jnp.where(q_role, q, kk) * e).astype(BF16))
        qk = q * kk
        for hg in range(HGRN_GROUP):
            hd = g * HGRN_GROUP + hg
            sl = slice(hd * B_DK, (hd + 1) * B_DK)
            gl = slice(hg * B_DK, (hg + 1) * B_DK)
            att = jnp.where(eye, jnp.sum(qk[:, gl], axis=-1, keepdims=True), 0.0)
            for li in range(len(levels)):
                x = xs[li][:, gl]
                att = att + mask_ref[li] * _dot_nt(x, x)
            vh = v_ref[:, sl]
            st = st_ref[hd]
            out = _dot(att.astype(BF16), vh) + _dot_nt(q_in[:, gl], st.astype(BF16))
            if final:
                og = og_ref[:, sl].astype(F32)
                o_ref[:, sl] = (_rms(out + of_ref[:, sl]) * (og * _sigmoid(og))).astype(BF16)
            else:
                o_ref[:, sl] = out
            st_ref[hd] = st * decay[:, gl] + _dot(vh.astype(F32).T.astype(BF16), k_st[:, gl])


def _hgrn(p, pf, lb, consts, n_batch, seq, ctx_len, col0, direction, o_fwd=None):
    t = p.shape[0]
    bw = lb.shape[1]
    m3, masks = consts
    n_rows = m3.shape[0]
    n_lev = masks.shape[0]
    block, steps = _chunk_block(n_batch, seq, ctx_len, LB, direction)
    final = o_fwd is not None
    spec = lambda colblk: pl.BlockSpec((LB, bw), lambda b, s: (block(b, s), colblk))
    const = lambda shape: pl.BlockSpec(shape, lambda b, s: (0,) * len(shape))
    in_specs = [spec(col0), spec(direction), spec(col0 + 1)]
    args = [p, pf, p]
    if final:
        in_specs += [spec(0), spec(col0 + 2)]
        args += [o_fwd, p]
    in_specs += [const((1, bw)), const((n_rows, B_PARTS * LB)), const((n_lev, LB, LB))]
    args += [lb, m3, masks]
    return pl.pallas_call(
        functools.partial(_hgrn_kernel, direction=direction, final=final),
        grid=(n_batch, steps),
        in_specs=in_specs,
        out_specs=spec(0),
        out_shape=jax.ShapeDtypeStruct((t, bw), BF16 if final else F32),
        scratch_shapes=[pltpu.VMEM((bw // B_DK, B_DK, B_DK), F32),
                        pltpu.VMEM((n_rows, bw), F32)],
        compiler_params=_cparams(("parallel", "arbitrary")),
        name="hgrn_bwd" if final else "hgrn_fwd",
    )(*args)


def _scan_rows(a, h, reverse):
    n = a.shape[0]
    idx = lax.broadcasted_iota(jnp.int32, (n, 1), 0)
    k = 1
    while k < n:
        if reverse:
            a_s, h_s, valid = pltpu.roll(a, n - k, 0), pltpu.roll(h, n - k, 0), idx < n - k
        else:
            a_s, h_s, valid = pltpu.roll(a, k, 0), pltpu.roll(h, k, 0), idx >= k
        h = jnp.where(valid, a * h_s + h, h)
        a = jnp.where(valid, a_s * a, a)
        k *= 2
    return a, h


def _shift_rows(x, delta):
    n = x.shape[0]
    idx = lax.broadcasted_iota(jnp.int32, (n, 1), 0)
    if delta == 0:
        return x
    y = pltpu.roll(x, (-delta) % n, 0)
    valid = (idx + delta >= 0) & (idx + delta < n)
    return jnp.where(valid, y, 0.0)


def _rg_gates(y, wg, gb, rate):
    c = y.shape[1]
    g = _dot(y.astype(BF16), wg) + gb
    out = []
    for d in range(2):
        r = _sigmoid(g[:, 2 * d * c:(2 * d + 1) * c])
        i = _sigmoid(g[:, (2 * d + 1) * c:(2 * d + 2) * c])
        a = jnp.exp2(r * rate[d:d + 1, :])
        out.append((a, jnp.sqrt(1.0 - a * a) * (i * y)))
    return out


def _rglru_kernel(xl_ref, xc_ref, gl_ref, gc_ref, cw_ref, cb_ref, wg_ref, gb_ref, lam_ref,
                  ol_ref, oc_ref, xe_ref, a_ref, bx_ref, hf_ref, *, n_rows):
    w = GRID_W
    seq, c = xl_ref.shape
    rate = (-RG_C * LOG2E) * _softplus(-lam_ref[...])
    cw = cw_ref[...]
    cb = cb_ref[...]
    wg = wg_ref[...]
    gb = gb_ref[...]

    xc = xc_ref[...].astype(F32)
    yc = cb + sum(cw[j:j + 1, :] * _shift_rows(xc, j - 2) for j in range(CONV_W))
    (a_f, bx_f), (a_b, bx_b) = _rg_gates(yc, wg, gb, rate)
    _, hs_f = _scan_rows(a_f, bx_f, reverse=False)
    _, hs_b = _scan_rows(a_b, bx_b, reverse=True)
    n_ctx = xc.shape[0]
    h0 = (hs_f[n_ctx - 1:n_ctx, :], hs_b[0:1, :])
    oc_ref[...] = ((hs_f + hs_b) * _gelu_tanh(gc_ref[...].astype(F32))).astype(BF16)

    xe_ref[2 * w:2 * w + seq, :] = xl_ref[...].astype(F32)
    xe_ref[0:w, :] = _shift_rows(xl_ref[(n_rows - 2) * w:(n_rows - 1) * w, :].astype(F32), -1)
    xe_ref[w:2 * w, :] = _shift_rows(xl_ref[(n_rows - 1) * w:n_rows * w, :].astype(F32), -1)
    xe_ref[2 * w + seq:3 * w + seq, :] = _shift_rows(xl_ref[0:w, :].astype(F32), 1)

    def gate_chunk(i, carry):
        base = pl.multiple_of(i * RG_ROWS, RG_ROWS)
        y = cb + sum(cw[j:j + 1, :] * xe_ref[pl.ds(base + j * w, RG_ROWS), :] for j in range(CONV_W))
        for d, (a, bx) in enumerate(_rg_gates(y, wg, gb, rate)):
            a_ref[d, pl.ds(base, RG_ROWS), :] = a
            bx_ref[d, pl.ds(base, RG_ROWS), :] = bx
        return carry

    lax.fori_loop(0, seq // RG_ROWS, gate_chunk, 0)

    for d in range(2):
        def slab(i):
            r = i if d == 0 else n_rows - 1 - i
            return pl.ds(pl.multiple_of(r * w, w), w)

        def column_totals(i, carry):
            a_tot, h_end = carry
            a = a_ref[d, slab(i), :]
            return a * a_tot, a * h_end + bx_ref[d, slab(i), :]

        a_tot, h_end = lax.fori_loop(0, n_rows, column_totals,
                                     (jnp.ones((w, c), F32), jnp.zeros((w, c), F32)), unroll=RG_UNROLL)
        a_cum, h_cum = _scan_rows(a_tot, h_end, reverse=(d == 1))
        after = a_cum * h0[d] + h_cum
        w_idx = lax.broadcasted_iota(jnp.int32, (w, 1), 0)
        if d == 0:
            h_in = jnp.where(w_idx == 0, h0[d], pltpu.roll(after, 1, 0))
        else:
            h_in = jnp.where(w_idx == w - 1, h0[d], pltpu.roll(after, w - 1, 0))

        def emit(i, h):
            h = a_ref[d, slab(i), :] * h + bx_ref[d, slab(i), :]
            if d == 0:
                hf_ref[slab(i), :] = h
            else:
                gate = _gelu_tanh(gl_ref[slab(i), :].astype(F32))
                ol_ref[slab(i), :] = ((hf_ref[slab(i), :] + h) * gate).astype(BF16)
            return h

        lax.fori_loop(0, n_rows, emit, h_in, unroll=RG_UNROLL)


def _rglru(p, conv_w, conv_b, wg, gb, lam, n_batch, seq, ctx_len, col0):
    cw_total = conv_w.shape[1]
    n_slabs = cw_total // RG_SLAB
    n_rows = seq // GRID_W
    ctx_base = n_batch * seq // ctx_len
    lat = lambda colblk: pl.BlockSpec((seq, RG_SLAB), lambda b, j: (b, colblk * n_slabs + j))
    ctx = lambda colblk: pl.BlockSpec((ctx_len, RG_SLAB), lambda b, j: (ctx_base + b, colblk * n_slabs + j))
    return pl.pallas_call(
        functools.partial(_rglru_kernel, n_rows=n_rows),
        grid=(n_batch, n_slabs),
        in_specs=[
            lat(col0), ctx(col0), lat(col0 + 1), ctx(col0 + 1),
            pl.BlockSpec((CONV_W, RG_SLAB), lambda b, j: (0, j)),
            pl.BlockSpec((1, RG_SLAB), lambda b, j: (0, j)),
            pl.BlockSpec((None, RG_SLAB, 4 * RG_SLAB), lambda b, j: (j, 0, 0)),
            pl.BlockSpec((None, 1, 4 * RG_SLAB), lambda b, j: (j, 0, 0)),
            pl.BlockSpec((2, RG_SLAB), lambda b, j: (0, j)),
        ],
        out_specs=[pl.BlockSpec((seq, RG_SLAB), lambda b, j: (b, j)),
                   pl.BlockSpec((ctx_len, RG_SLAB), lambda b, j: (b, j))],
        out_shape=[jax.ShapeDtypeStruct((n_batch * seq, cw_total), BF16),
                   jax.ShapeDtypeStruct((n_batch * ctx_len, cw_total), BF16)],
        scratch_shapes=[pltpu.VMEM((seq + 3 * GRID_W, RG_SLAB), F32),
                        pltpu.VMEM((2, seq, RG_SLAB), F32), pltpu.VMEM((2, seq, RG_SLAB), F32),
                        pltpu.VMEM((seq, RG_SLAB), F32)],
        compiler_params=_cparams(("parallel", "arbitrary")),
        name="rglru",
    )(p, p, p, p, conv_w, conv_b, wg, gb, lam)


def _merge_kernel(ya_ref, yb_ref, ycl_ref, ycc_ref, ga_ref, gb_ref, gc_ref, h_ref, mod_ref, bw_ref,
                  wo_ref, o_ref, *, mod_base, n_lat_tiles):
    j = pl.program_id(1)

    @pl.when(j == 0)
    def _():
        o_ref[...] = jnp.zeros_like(o_ref)

    yc = jnp.where(pl.program_id(0) < n_lat_tiles, ycl_ref[...], ycc_ref[...])
    merged = (_sigmoid(ga_ref[...].astype(F32)) * _dot(ya_ref[...], bw_ref[0])
              + _sigmoid(gb_ref[...].astype(F32)) * _dot(yb_ref[...], bw_ref[1])
              + _sigmoid(gc_ref[...].astype(F32)) * _dot(yc, bw_ref[2]))
    o_ref[...] += _dot(merged.astype(BF16), wo_ref[...])

    @pl.when(j == pl.num_programs(1) - 1)
    def _():
        o_ref[...] = h_ref[...] + mod_ref[mod_base:mod_base + 1, :] * o_ref[...]


def _merge(h, n_rows, p, ya, yb, yc_lat, yc_ctx, modtab, mod_row, branch_w, w_out, layer, mod_base):
    d = h.shape[1]
    bw = ya.shape[1]
    n_j = d // MERGE_TILE
    gate0 = 9 * bw // MERGE_TILE
    per_branch = d // MERGE_TILE
    n_lat_tiles = yc_lat.shape[0] // MERGE_ROWS
    branch = pl.BlockSpec((MERGE_ROWS, bw), lambda i, j: (i, 0))
    branch_lat = pl.BlockSpec((MERGE_ROWS, bw), lambda i, j: (jnp.minimum(i, n_lat_tiles - 1), 0))
    branch_ctx = pl.BlockSpec((MERGE_ROWS, bw), lambda i, j: (jnp.maximum(i - n_lat_tiles, 0), 0))
    gate = lambda k: pl.BlockSpec((MERGE_ROWS, MERGE_TILE), lambda i, j: (i, gate0 + k * per_branch + j))
    return pl.pallas_call(
        functools.partial(_merge_kernel, mod_base=mod_base, n_lat_tiles=n_lat_tiles),
        grid=(n_rows // MERGE_ROWS, n_j),
        in_specs=[branch, branch, branch_lat, branch_ctx, gate(0), gate(1), gate(2),
                  pl.BlockSpec((MERGE_ROWS, d), lambda i, j: (i, 0)),
                  pl.BlockSpec((None, N_MOD, d), lambda i, j: (mod_row(MERGE_ROWS)(i), 0, 0)),
                  pl.BlockSpec((None, 3, bw, MERGE_TILE), lambda i, j: (layer, 0, 0, j)),
                  pl.BlockSpec((None, MERGE_TILE, d), lambda i, j: (layer, j, 0))],
        out_specs=pl.BlockSpec((MERGE_ROWS, d), lambda i, j: (i, 0)),
        out_shape=jax.ShapeDtypeStruct((n_rows, d), F32),
        compiler_params=_cparams(("parallel", "arbitrary")),
        name="merge",
    )(ya, yb, yc_lat, yc_ctx, p, p, p, h, modtab, branch_w, w_out)


def kernel(x, c, ctx, c_ctx, mod_w, mod_b, norm_g, ffn_w_in, ffn_w_out, w_in, mlstm_gate_b,
           hgrn_lb_logits, conv_w, conv_b, rg_gate_w, rg_gate_b, rg_lambda, branch_w, w_out, final_g):
    n_batch, seq, d = x.shape
    ctx_len = ctx.shape[1]
    depth = mod_w.shape[0]
    aw = branch_w.shape[2]
    dh = aw // A_HEADS
    n_lat_rows = n_batch * seq
    n_rows = n_lat_rows + n_batch * ctx_len
    assert seq % ROW_TILE == 0 and (n_batch * ctx_len) % ROW_TILE == 0
    assert seq % LA == 0 and ctx_len % LA == 0 and seq % LB == 0 and ctx_len % LB == 0
    assert seq % GRID_W == 0 and seq % RG_ROWS == 0 and n_lat_rows % ctx_len == 0
    assert n_batch + 1 <= MOD_ROWS and aw % RG_SLAB == 0 and RG_SLAB % (aw // C_BLOCKS) == 0

    def mod_row(tile):
        return lambda i: jnp.where(i < n_lat_rows // tile, 1 + i // (seq // tile), 0)

    c16 = jnp.zeros((MOD_ROWS, d), F32).at[0].set(c_ctx).at[1:1 + n_batch].set(c)
    modtab = _mod_table(c16, mod_w, mod_b).reshape(depth, MOD_ROWS, N_MOD, d)
    ffn_w_in_b = ffn_w_in.astype(BF16)
    ffn_w_out_b = ffn_w_out.astype(BF16)
    g0 = 4 * aw
    n_gate = 4 * A_HEADS
    b0 = g0 + n_gate
    w_in_b = w_in.astype(BF16)
    w_main = jnp.concatenate([w_in_b[:, :, :g0], w_in_b[:, :, b0:b0 + aw], w_in_b[:, :, b0 + 3 * aw:],
                              w_in_b[:, :, b0 + aw:b0 + 3 * aw]], axis=-1)
    n16 = (w_main.shape[2] - 2 * aw) // PROJ_TILE
    w_gate = jnp.pad(w_in_b[:, :, g0:g0 + n_gate], ((0, 0), (0, 0), (0, GATE_PAD - n_gate)))
    gate_bias = jnp.pad(mlstm_gate_b.reshape(depth, 1, n_gate), ((0, 0), (0, 0), (0, GATE_PAD - n_gate)))
    lb_p = jax.nn.softmax(hgrn_lb_logits.astype(F32), axis=0)
    lb_all = jnp.cumsum(lb_p, axis=0) - lb_p[0:1]
    branch_w_b = branch_w.astype(BF16)
    w_out_b = w_out.astype(BF16)
    n_slabs = aw // RG_SLAB
    c_db = aw // C_BLOCKS
    per_slab = RG_SLAB // c_db
    blocks = rg_gate_w.reshape(depth, 2, 2, n_slabs, per_slab, c_db, c_db)
    eye = jnp.eye(per_slab, dtype=F32)
    dense = jnp.einsum('ldgspio,pq->ldgspiqo', blocks, eye).reshape(depth, 2, 2, n_slabs, RG_SLAB, RG_SLAB)
    rg_w = dense.transpose(0, 3, 4, 1, 2, 5).reshape(depth, n_slabs, RG_SLAB, 4 * RG_SLAB).astype(BF16)
    rg_b = (rg_gate_b.reshape(depth, 2, 2, n_slabs, RG_SLAB).transpose(0, 3, 1, 2, 4)
            .reshape(depth, n_slabs, 1, 4 * RG_SLAB))
    tri = np.tril(np.ones((LA, LA), np.float32))
    tri3 = [jnp.asarray(np.concatenate([m] * A_PARTS, axis=1), BF16) for m in (tri, tri.T)]
    hgrn_consts = []
    for direction in (0, 1):
        m3, masks = _hgrn_constants(LB, direction)
        hgrn_consts.append((jnp.asarray(m3, BF16), jnp.asarray(masks, F32)))

    h = jnp.concatenate([x.reshape(n_lat_rows, d), ctx.reshape(n_batch * ctx_len, d)], axis=0)
    for layer in range(depth):
        last = layer == depth - 1
        mt = modtab[layer]
        ng = norm_g[layer]
        h = _ffn(h, n_rows, mt, mod_row, ng[0:1], ffn_w_in_b, ffn_w_out_b, layer, 0, 0)
        p, pf, gates = _inproj(h, mt, mod_row, ng[1:2], w_main, w_gate, layer, 3, n16)
        a_fwd = _mlstm(p, gates, gate_bias[layer], tri3[0], n_batch, seq, ctx_len, dh, 0)
        ya = _mlstm(p, gates, gate_bias[layer], tri3[1], n_batch, seq, ctx_len, dh, 1, o_fwd=a_fwd)
        b_fwd = _hgrn(p, pf, lb_all[layer, 0:1], hgrn_consts[0], n_batch, seq, ctx_len, 4, 0)
        yb = _hgrn(p, pf, lb_all[layer, 1:2], hgrn_consts[1], n_batch, seq, ctx_len, 4, 1, o_fwd=b_fwd)
        yc_lat, yc_ctx = _rglru(p, conv_w[layer], conv_b[layer][None, :], rg_w[layer], rg_b[layer],
                                rg_lambda[layer], n_batch, seq, ctx_len, 7)
        rows_out = n_lat_rows if last else n_rows
        h = _merge(h, rows_out, p, ya, yb, yc_lat, yc_ctx, mt, mod_row, branch_w_b, w_out_b, layer, 5)
        h = _ffn(h, rows_out, mt, mod_row, ng[2:3], ffn_w_in_b, ffn_w_out_b, layer, 1, 6,
                 final_g=final_g[None, :] if last else None)
    return h.reshape(n_batch, seq, d)
```

```python
import functools

import numpy as np
import jax
import jax.numpy as jnp
from jax import lax
from jax.experimental import pallas as pl
from jax.experimental.pallas import tpu as pltpu

EPS = 1e-6
N_MOD = 9
GRID_W = 64
A_HEADS = 4
B_DK = 128
C_BLOCKS = 16
CONV_W = 4
RG_C = 8.0
NEG_BIG = -1e30
LOG2E = 1.4426950408889634

MOD_ROWS = 16
GATE_PAD = 128
ROW_TILE = 1024
FF_TILE = 512
FFN_SUB = 512
NORM_STRIP = 16
PROJ_TILE = 1024
MOD_TILE = 1024
MERGE_TILE = 1024
MERGE_ROWS = 512
LA = 256
LB = 128
A_PARTS = 3
B_PARTS = 2
HGRN_GROUP = 8
RG_SLAB = 256
RG_ROWS = 512
RG_UNROLL = 4
VMEM_LIMIT = 60 * 1024 * 1024

F32 = jnp.float32
BF16 = jnp.bfloat16


def _cparams(sem):
    return pltpu.CompilerParams(dimension_semantics=sem, vmem_limit_bytes=VMEM_LIMIT)


def _dot(a, b):
    return jnp.dot(a, b, preferred_element_type=F32)


def _dot_nt(a, b):
    return lax.dot_general(a, b, (((1,), (1,)), ((), ())), preferred_element_type=F32)


def _dot01(m_rep, x):
    parts = m_rep.shape[1] // x.shape[0]
    pieces = []
    rest = x
    for _ in range(parts):
        piece = rest.astype(BF16)
        pieces.append(piece)
        rest = rest - piece.astype(F32)
    return _dot(m_rep, jnp.concatenate(pieces, axis=0))


def _sigmoid(x):
    return 0.5 * jnp.tanh(0.5 * x) + 0.5


def _softplus(z):
    return jnp.maximum(z, 0.0) + jnp.log1p(jnp.exp(-jnp.abs(z)))


def _gelu_tanh(x):
    return 0.5 * x * (1.0 + jnp.tanh(np.sqrt(2.0 / np.pi).astype(np.float32) * (x + 0.044715 * (x * x * x))))


def _rms(x):
    return x * lax.rsqrt(jnp.mean(x * x, axis=-1, keepdims=True) + EPS)


def _mod_kernel(c_ref, w_ref, b_ref, o_ref):
    cv = c_ref[...]
    sc = (cv * _sigmoid(cv)).astype(BF16)
    o_ref[...] = _dot(sc, w_ref[...].astype(BF16)) + b_ref[...]


def _mod_table(c16, mod_w, mod_b):
    depth, d, n = mod_w.shape
    return pl.pallas_call(
        _mod_kernel,
        grid=(depth, n // MOD_TILE),
        in_specs=[
            pl.BlockSpec((MOD_ROWS, d), lambda l, j: (0, 0)),
            pl.BlockSpec((None, d, MOD_TILE), lambda l, j: (l, 0, j)),
            pl.BlockSpec((None, 1, MOD_TILE), lambda l, j: (l, 0, j)),
        ],
        out_specs=pl.BlockSpec((None, MOD_ROWS, MOD_TILE), lambda l, j: (l, 0, j)),
        out_shape=jax.ShapeDtypeStruct((depth, MOD_ROWS, n), F32),
        compiler_params=_cparams(("arbitrary", "arbitrary")),
        name="mod_table",
    )(c16, mod_w, mod_b.reshape(depth, 1, n))


def _ffn_kernel(h_ref, mod_ref, g_ref, wg_ref, wu_ref, wo_ref, *rest, mod_base, final):
    if final:
        fg_ref, o_ref, xn_ref = rest
    else:
        o_ref, xn_ref = rest
    j = pl.program_id(1)
    subs = [slice(r, r + FFN_SUB) for r in range(0, h_ref.shape[0], FFN_SUB)]

    @pl.when(j == 0)
    def _():
        gain = g_ref[...] * (1.0 + mod_ref[mod_base + 1:mod_base + 2, :])
        shift = mod_ref[mod_base:mod_base + 1, :]

        def strip(r, carry):
            rows = pl.ds(pl.multiple_of(r * NORM_STRIP, NORM_STRIP), NORM_STRIP)
            xn_ref[rows, :] = (_rms(h_ref[rows, :]) * gain + shift).astype(BF16)
            o_ref[rows, :] = jnp.zeros((NORM_STRIP, o_ref.shape[1]), F32)
            return carry

        lax.fori_loop(0, h_ref.shape[0] // NORM_STRIP, strip, 0, unroll=8)

    for rows in subs:
        xn = xn_ref[rows, :]
        gate = _dot(xn, wg_ref[...])
        up = _dot(xn, wu_ref[...])
        act = (gate * _sigmoid(gate)) * up
        o_ref[rows, :] += _dot(act.astype(BF16), wo_ref[...])

    @pl.when(j == pl.num_programs(1) - 1)
    def _():
        for rows in subs:
            out = h_ref[rows, :] + (0.5 * mod_ref[mod_base + 2:mod_base + 3, :]) * o_ref[rows, :]
            if final:
                out = _rms(out) * fg_ref[...]
            o_ref[rows, :] = out


def _ffn(h, n_rows, modtab, mod_row, g, w_in, w_out, layer, which, mod_base, final_g=None):
    d = h.shape[1]
    d_ff = w_out.shape[2]
    n_ff = d_ff // FF_TILE
    final = final_g is not None
    in_specs = [
        pl.BlockSpec((ROW_TILE, d), lambda i, j: (i, 0)),
        pl.BlockSpec((None, N_MOD, d), lambda i, j: (mod_row(ROW_TILE)(i), 0, 0)),
        pl.BlockSpec((1, d), lambda i, j: (0, 0)),
        pl.BlockSpec((None, None, d, FF_TILE), lambda i, j: (layer, which, 0, j)),
        pl.BlockSpec((None, None, d, FF_TILE), lambda i, j: (layer, which, 0, j + n_ff)),
        pl.BlockSpec((None, None, FF_TILE, d), lambda i, j: (layer, which, j, 0)),
    ]
    args = [h, modtab, g, w_in, w_in, w_out]
    if final:
        in_specs.append(pl.BlockSpec((1, d), lambda i, j: (0, 0)))
        args.append(final_g)
    return pl.pallas_call(
        functools.partial(_ffn_kernel, mod_base=mod_base, final=final),
        grid=(n_rows // ROW_TILE, n_ff),
        in_specs=in_specs,
        out_specs=pl.BlockSpec((ROW_TILE, d), lambda i, j: (i, 0)),
        out_shape=jax.ShapeDtypeStruct((n_rows, d), F32),
        scratch_shapes=[pltpu.VMEM((ROW_TILE, d), BF16)],
        compiler_params=_cparams(("parallel", "arbitrary")),
        name="ffn",
    )(*args)


def _inproj_kernel(h_ref, mod_ref, g_ref, w_ref, wgate_ref, p16_ref, pf_ref, gate_ref, xn_ref,
                   *, mod_base, n16):
    j = pl.program_id(1)

    @pl.when(j == 0)
    def _():
        gain = g_ref[...] * (1.0 + mod_ref[mod_base + 1:mod_base + 2, :])
        shift = mod_ref[mod_base:mod_base + 1, :]

        xn = (_rms(h_ref[...]) * gain + shift).astype(BF16)
        xn_ref[...] = xn
        gate_ref[...] = _dot(xn, wgate_ref[...])

    @pl.when(j < n16)
    def _():
        p16_ref[...] = _dot(xn_ref[...], w_ref[...]).astype(BF16)

    @pl.when(j >= n16)
    def _():
        pf_ref[...] = _dot(xn_ref[...], w_ref[...])


def _inproj(h, modtab, mod_row, g, w_main, w_gate, layer, mod_base, n16):
    t, d = h.shape
    n = w_main.shape[2]
    n_tiles = n // PROJ_TILE
    return pl.pallas_call(
        functools.partial(_inproj_kernel, mod_base=mod_base, n16=n16),
        grid=(t // ROW_TILE, n_tiles),
        in_specs=[
            pl.BlockSpec((ROW_TILE, d), lambda i, j: (i, 0)),
            pl.BlockSpec((None, N_MOD, d), lambda i, j: (mod_row(ROW_TILE)(i), 0, 0)),
            pl.BlockSpec((1, d), lambda i, j: (0, 0)),
            pl.BlockSpec((None, d, PROJ_TILE), lambda i, j: (layer, 0, j)),
            pl.BlockSpec((None, d, GATE_PAD), lambda i, j: (layer, 0, 0)),
        ],
        out_specs=[
            pl.BlockSpec((ROW_TILE, PROJ_TILE), lambda i, j: (i, jnp.minimum(j, n16 - 1))),
            pl.BlockSpec((ROW_TILE, PROJ_TILE), lambda i, j: (i, jnp.maximum(j - n16, 0))),
            pl.BlockSpec((ROW_TILE, GATE_PAD), lambda i, j: (i, 0)),
        ],
        out_shape=[jax.ShapeDtypeStruct((t, n16 * PROJ_TILE), BF16),
                   jax.ShapeDtypeStruct((t, (n_tiles - n16) * PROJ_TILE), F32),
                   jax.ShapeDtypeStruct((t, GATE_PAD), F32)],
        scratch_shapes=[pltpu.VMEM((ROW_TILE, d), BF16)],
        compiler_params=_cparams(("parallel", "arbitrary")),
        name="inproj",
    )(h, modtab, g, w_main, w_gate)


def _chunk_block(n_batch, seq, ctx_len, length, direction):
    n_lat = seq // length
    n_ctx = ctx_len // length
    ctx_base = n_batch * seq // length

    def block(b, s):
        if direction == 0:
            return jnp.where(s < n_ctx, ctx_base + b * n_ctx + s, b * n_lat + (s - n_ctx))
        return jnp.where(s < n_ctx, ctx_base + b * n_ctx + (n_ctx - 1 - s),
                         b * n_lat + (n_lat - 1 - (s - n_ctx)))

    return block, n_ctx + n_lat


def _mlstm_kernel(q_ref, k_ref, v_ref, g_ref, *rest, dh, direction, final):
    if final:
        of_ref, og_ref, bias_ref, tri_ref, o_ref, c_ref, n_ref, m_ref = rest
    else:
        bias_ref, tri_ref, o_ref, c_ref, n_ref, m_ref = rest
    length = q_ref.shape[0]

    @pl.when(pl.program_id(1) == 0)
    def _():
        c_ref[...] = jnp.zeros_like(c_ref)
        n_ref[...] = jnp.zeros_like(n_ref)
        m_ref[...] = jnp.zeros_like(m_ref)

    row = lax.broadcasted_iota(jnp.int32, (length, length), 0)
    col = lax.broadcasted_iota(jnp.int32, (length, length), 1)
    mask = (row <= col) if direction == 0 else (row >= col)
    last = length - 1 if direction == 0 else 0
    pre = g_ref[...] + bias_ref[...]
    logf = jnp.minimum(pre, 0.0) - jnp.log1p(jnp.exp(-jnp.abs(pre)))
    b_all = _dot01(tri_ref[...], logf)
    b_all_t = b_all.T
    pre_t = pre.T
    for hh in range(A_HEADS):
        ci = direction * 2 * A_HEADS + hh
        cf = ci + A_HEADS
        sl = slice(hh * dh, (hh + 1) * dh)
        b_row = b_all_t[cf:cf + 1, :]
        r_row = pre_t[ci:ci + 1, :] - b_row
        r_col = pre[:, ci:ci + 1] - b_all[:, cf:cf + 1]
        m_prev = m_ref[hh][:, 0:1]
        c_t = c_ref[hh]
        nvec = n_ref[hh]

        dmat = jnp.where(mask, r_col + b_row, NEG_BIG)
        inter = b_row + m_prev
        m_t = jnp.maximum(inter, jnp.max(dmat, axis=0, keepdims=True))
        w_inter = jnp.exp(inter - m_t)
        q = q_ref[:, sl]
        k = (k_ref[:, sl].astype(F32) * (dh ** -0.5)).astype(BF16)
        v_t = v_ref[:, sl].astype(F32).T
        s_t = _dot_nt(k, q) * jnp.exp(dmat - m_t)
        num_t = w_inter * _dot_nt(c_t.astype(BF16), q) + _dot(v_t.astype(BF16), s_t.astype(BF16))
        n_rows = jnp.broadcast_to(nvec, (16, dh)).astype(BF16)
        den = w_inter * _dot_nt(n_rows, q)[0:1, :] + jnp.sum(s_t, axis=0, keepdims=True)
        out = (num_t / jnp.maximum(jnp.abs(den), jnp.exp(-m_t))).T
        if final:
            o_ref[:, sl] = (_rms(out + of_ref[:, sl]) * _sigmoid(og_ref[:, sl].astype(F32))).astype(BF16)
        else:
            o_ref[:, sl] = out

        b_last = b_all_t[cf:cf + 1, last:last + 1]
        wlog = b_last + r_row
        m_new = jnp.maximum(b_last + m_prev, jnp.max(wlog, axis=-1, keepdims=True))
        decay = jnp.exp(b_last + m_prev - m_new)
        ws = jnp.exp(wlog - m_new)
        c_ref[hh] = decay * c_t + _dot((v_t * ws).astype(BF16), k)
        ws_rows = jnp.broadcast_to(ws, (16, length)).astype(BF16)
        n_ref[hh] = decay * nvec + _dot(ws_rows, k)[0:1, :]
        m_ref[hh] = jnp.broadcast_to(m_new, m_ref.shape[1:])


def _mlstm(p, gates, gate_bias, tri3, n_batch, seq, ctx_len, dh, direction, o_fwd=None):
    t = p.shape[0]
    aw = A_HEADS * dh
    block, steps = _chunk_block(n_batch, seq, ctx_len, LA, direction)
    final = o_fwd is not None
    spec = lambda colblk, width: pl.BlockSpec((LA, width), lambda b, s: (block(b, s), colblk))
    const = lambda shape: pl.BlockSpec(shape, lambda b, s: (0,) * len(shape))
    in_specs = [spec(0, aw), spec(1, aw), spec(2, aw), spec(0, GATE_PAD)]
    args = [p, p, p, gates]
    if final:
        in_specs += [spec(0, aw), spec(3, aw)]
        args += [o_fwd, p]
    in_specs += [const((1, GATE_PAD)), const((LA, A_PARTS * LA))]
    args += [gate_bias, tri3]
    return pl.pallas_call(
        functools.partial(_mlstm_kernel, dh=dh, direction=direction, final=final),
        grid=(n_batch, steps),
        in_specs=in_specs,
        out_specs=spec(0, aw),
        out_shape=jax.ShapeDtypeStruct((t, aw), BF16 if final else F32),
        scratch_shapes=[pltpu.VMEM((A_HEADS, dh, dh), F32),
                        pltpu.VMEM((A_HEADS, 1, dh), F32),
                        pltpu.VMEM((A_HEADS, 1, 128), F32)],
        compiler_params=_cparams(("parallel", "arbitrary")),
        name="mlstm_bwd" if final else "mlstm_fwd",
    )(*args)


def _hgrn_levels(length):
    levels = []
    c = length // 2
    while c >= 1:
        levels.append(c)
        c //= 2
    return levels


def _hgrn_constants(length, direction):
    t = np.arange(length)[:, None]
    u = np.arange(length)[None, :]
    if direction == 0:
        mats = [u <= t, u > t]
    else:
        mats = [u >= t, u < t]
    masks = []
    for c in _hgrn_levels(length):
        blk = t // (2 * c)
        pos = t % (2 * c)
        if direction == 0:
            ref = blk * 2 * c + c - 1
            q_role = pos >= c
            m = np.where(q_role, (u > ref) & (u <= t), (u > t) & (u <= ref))
            keep = (blk == blk.T) & q_role & (pos.T < c)
        else:
            ref = blk * 2 * c + c
            q_role = pos < c
            m = np.where(q_role, (u >= t) & (u < ref), (u >= ref) & (u < t))
            keep = (blk == blk.T) & q_role & (pos.T >= c)
        mats.append(m)
        masks.append(keep)
    m_all = np.concatenate(mats, 0).astype(np.float32)
    return np.concatenate([m_all] * B_PARTS, axis=1), np.stack(masks).astype(np.float32)


def _hgrn_kernel(q_ref, f_ref, v_ref, *rest, direction, final):
    if final:
        of_ref, og_ref, lb_ref, m3_ref, mask_ref, o_ref, st_ref, d_ref = rest
    else:
        lb_ref, m3_ref, mask_ref, o_ref, st_ref, d_ref = rest
    length, width = q_ref.shape
    n_heads = width // B_DK
    levels = _hgrn_levels(length)
    last = length - 1 if direction == 0 else 0

    @pl.when(pl.program_id(1) == 0)
    def _():
        st_ref[...] = jnp.zeros_like(st_ref)

    t_idx = lax.broadcasted_iota(jnp.int32, (length, 1), 0)
    eye = (lax.broadcasted_iota(jnp.int32, (length, length), 0)
           == lax.broadcasted_iota(jnp.int32, (length, length), 1))
    gw = HGRN_GROUP * B_DK
    for g in range(n_heads // HGRN_GROUP):
        cols = slice(g * gw, (g + 1) * gw)
        lb = lb_ref[:, cols]
        f = lb + (1.0 - lb) * _sigmoid(f_ref[:, cols])
        kk = 1.0 - f
        qpre = q_ref[:, cols].astype(F32)
        q = qpre * _sigmoid(qpre)
        d_ref[:, cols] = _dot01(m3_ref[...], jnp.log(f) * LOG2E)
        q_in = (q * jnp.exp2(d_ref[0:length, cols])).astype(BF16)
        k_st = (kk * jnp.exp2(d_ref[length:2 * length, cols])).astype(BF16)
        decay = jnp.exp2(d_ref[last:last + 1, cols])
        xs = []
        for li, c in enumerate(levels):
            pos = t_idx % (2 * c)
            q_role = (pos >= c) if direction == 0 else (pos < c)
            e = jnp.exp2(d_ref[(2 + li) * length:(3 + li) * length, cols])
            xs.append((jnp.where(q_role, q, kk) * e).astype(BF16))
        qk = q * kk
        for hg in range(HGRN_GROUP):
            hd = g * HGRN_GROUP + hg
            sl = slice(hd * B_DK, (hd + 1) * B_DK)
            gl = slice(hg * B_DK, (hg + 1) * B_DK)
            att = jnp.where(eye, jnp.sum(qk[:, gl], axis=-1, keepdims=True), 0.0)
            for li in range(len(levels)):
                x = xs[li][:, gl]
                att = att + mask_ref[li] * _dot_nt(x, x)
            vh = v_ref[:, sl]
            st = st_ref[hd]
            out = _dot(att.astype(BF16), vh) + _dot_nt(q_in[:, gl], st.astype(BF16))
            if final:
                og = og_ref[:, sl].astype(F32)
                o_ref[:, sl] = (_rms(out + of_ref[:, sl]) * (og * _sigmoid(og))).astype(BF16)
            else:
                o_ref[:, sl] = out
            st_ref[hd] = st * decay[:, gl] + _dot(vh.astype(F32).T.astype(BF16), k_st[:, gl])


def _hgrn(p, pf, lb, consts, n_batch, seq, ctx_len, col0, direction, o_fwd=None):
    t = p.shape[0]
    bw = lb.shape[1]
    m3, masks = consts
    n_rows = m3.shape[0]
    n_lev = masks.shape[0]
    block, steps = _chunk_block(n_batch, seq, ctx_len, LB, direction)
    final = o_fwd is not None
    spec = lambda colblk: pl.BlockSpec((LB, bw), lambda b, s: (block(b, s), colblk))
    const = lambda shape: pl.BlockSpec(shape, lambda b, s: (0,) * len(shape))
    in_specs = [spec(col0), spec(direction), spec(col0 + 1)]
    args = [p, pf, p]
    if final:
        in_specs += [spec(0), spec(col0 + 2)]
        args += [o_fwd, p]
    in_specs += [const((1, bw)), const((n_rows, B_PARTS * LB)), const((n_lev, LB, LB))]
    args += [lb, m3, masks]
    return pl.pallas_call(
        functools.partial(_hgrn_kernel, direction=direction, final=final),
        grid=(n_batch, steps),
        in_specs=in_specs,
        out_specs=spec(0),
        out_shape=jax.ShapeDtypeStruct((t, bw), BF16 if final else F32),
        scratch_shapes=[pltpu.VMEM((bw // B_DK, B_DK, B_DK), F32),
                        pltpu.VMEM((n_rows, bw), F32)],
        compiler_params=_cparams(("parallel", "arbitrary")),
        name="hgrn_bwd" if final else "hgrn_fwd",
    )(*args)


def _scan_rows(a, h, reverse):
    n = a.shape[0]
    idx = lax.broadcasted_iota(jnp.int32, (n, 1), 0)
    k = 1
    while k < n:
        if reverse:
            a_s, h_s, valid = pltpu.roll(a, n - k, 0), pltpu.roll(h, n - k, 0), idx < n - k
        else:
            a_s, h_s, valid = pltpu.roll(a, k, 0), pltpu.roll(h, k, 0), idx >= k
        h = jnp.where(valid, a * h_s + h, h)
        a = jnp.where(valid, a_s * a, a)
        k *= 2
    return a, h


def _shift_rows(x, delta):
    n = x.shape[0]
    idx = lax.broadcasted_iota(jnp.int32, (n, 1), 0)
    if delta == 0:
        return x
    y = pltpu.roll(x, (-delta) % n, 0)
    valid = (idx + delta >= 0) & (idx + delta < n)
    return jnp.where(valid, y, 0.0)


def _rg_gates(y, wg, gb, rate):
    c = y.shape[1]
    g = _dot(y.astype(BF16), wg) + gb
    out = []
    for d in range(2):
        r = _sigmoid(g[:, 2 * d * c:(2 * d + 1) * c])
        i = _sigmoid(g[:, (2 * d + 1) * c:(2 * d + 2) * c])
        a = jnp.exp2(r * rate[d:d + 1, :])
        out.append((a, jnp.sqrt(1.0 - a * a) * (i * y)))
    return out


def _rglru_kernel(xl_ref, xc_ref, gl_ref, gc_ref, cw_ref, cb_ref, wg_ref, gb_ref, lam_ref,
                  ol_ref, oc_ref, xe_ref, a_ref, bx_ref, hf_ref, *, n_rows):
    w = GRID_W
    seq, c = xl_ref.shape
    rate = (-RG_C * LOG2E) * _softplus(-lam_ref[...])
    cw = cw_ref[...]
    cb = cb_ref[...]
    wg = wg_ref[...]
    gb = gb_ref[...]

    xc = xc_ref[...].astype(F32)
    yc = cb + sum(cw[j:j + 1, :] * _shift_rows(xc, j - 2) for j in range(CONV_W))
    (a_f, bx_f), (a_b, bx_b) = _rg_gates(yc, wg, gb, rate)
    _, hs_f = _scan_rows(a_f, bx_f, reverse=False)
    _, hs_b = _scan_rows(a_b, bx_b, reverse=True)
    n_ctx = xc.shape[0]
    h0 = (hs_f[n_ctx - 1:n_ctx, :], hs_b[0:1, :])
    oc_ref[...] = ((hs_f + hs_b) * _gelu_tanh(gc_ref[...].astype(F32))).astype(BF16)

    xe_ref[2 * w:2 * w + seq, :] = xl_ref[...].astype(F32)
    xe_ref[0:w, :] = _shift_rows(xl_ref[(n_rows - 2) * w:(n_rows - 1) * w, :].astype(F32), -1)
    xe_ref[w:2 * w, :] = _shift_rows(xl_ref[(n_rows - 1) * w:n_rows * w, :].astype(F32), -1)
    xe_ref[2 * w + seq:3 * w + seq, :] = _shift_rows(xl_ref[0:w, :].astype(F32), 1)

    def gate_chunk(i, carry):
        base = pl.multiple_of(i * RG_ROWS, RG_ROWS)
        y = cb + sum(cw[j:j + 1, :] * xe_ref[pl.ds(base + j * w, RG_ROWS), :] for j in range(CONV_W))
        for d, (a, bx) in enumerate(_rg_gates(y, wg, gb, rate)):
            a_ref[d, pl.ds(base, RG_ROWS), :] = a
            bx_ref[d, pl.ds(base, RG_ROWS), :] = bx
        return carry

    lax.fori_loop(0, seq // RG_ROWS, gate_chunk, 0)

    for d in range(2):
        def slab(i):
            r = i if d == 0 else n_rows - 1 - i
            return pl.ds(pl.multiple_of(r * w, w), w)

        def column_totals(i, carry):
            a_tot, h_end = carry
            a = a_ref[d, slab(i), :]
            return a * a_tot, a * h_end + bx_ref[d, slab(i), :]

        a_tot, h_end = lax.fori_loop(0, n_rows, column_totals,
                                     (jnp.ones((w, c), F32), jnp.zeros((w, c), F32)), unroll=RG_UNROLL)
        a_cum, h_cum = _scan_rows(a_tot, h_end, reverse=(d == 1))
        after = a_cum * h0[d] + h_cum
        w_idx = lax.broadcasted_iota(jnp.int32, (w, 1), 0)
        if d == 0:
            h_in = jnp.where(w_idx == 0, h0[d], pltpu.roll(after, 1, 0))
        else:
            h_in = jnp.where(w_idx == w - 1, h0[d], pltpu.roll(after, w - 1, 0))

        def emit(i, h):
            h = a_ref[d, slab(i), :] * h + bx_ref[d, slab(i), :]
            if d == 0:
                hf_ref[slab(i), :] = h
            else:
                gate = _gelu_tanh(gl_ref[slab(i), :].astype(F32))
                ol_ref[slab(i), :] = ((hf_ref[slab(i), :] + h) * gate).astype(BF16)
            return h

        lax.fori_loop(0, n_rows, emit, h_in, unroll=RG_UNROLL)


def _rglru(p, conv_w, conv_b, wg, gb, lam, n_batch, seq, ctx_len, col0):
    cw_total = conv_w.shape[1]
    n_slabs = cw_total // RG_SLAB
    n_rows = seq // GRID_W
    ctx_base = n_batch * seq // ctx_len
    lat = lambda colblk: pl.BlockSpec((seq, RG_SLAB), lambda b, j: (b, colblk * n_slabs + j))
    ctx = lambda colblk: pl.BlockSpec((ctx_len, RG_SLAB), lambda b, j: (ctx_base + b, colblk * n_slabs + j))
    return pl.pallas_call(
        functools.partial(_rglru_kernel, n_rows=n_rows),
        grid=(n_batch, n_slabs),
        in_specs=[
            lat(col0), ctx(col0), lat(col0 + 1), ctx(col0 + 1),
            pl.BlockSpec((CONV_W, RG_SLAB), lambda b, j: (0, j)),
            pl.BlockSpec((1, RG_SLAB), lambda b, j: (0, j)),
            pl.BlockSpec((None, RG_SLAB, 4 * RG_SLAB), lambda b, j: (j, 0, 0)),
            pl.BlockSpec((None, 1, 4 * RG_SLAB), lambda b, j: (j, 0, 0)),
            pl.BlockSpec((2, RG_SLAB), lambda b, j: (0, j)),
        ],
        out_specs=[pl.BlockSpec((seq, RG_SLAB), lambda b, j: (b, j)),
                   pl.BlockSpec((ctx_len, RG_SLAB), lambda b, j: (b, j))],
        out_shape=[jax.ShapeDtypeStruct((n_batch * seq, cw_total), BF16),
                   jax.ShapeDtypeStruct((n_batch * ctx_len, cw_total), BF16)],
        scratch_shapes=[pltpu.VMEM((seq + 3 * GRID_W, RG_SLAB), F32),
                        pltpu.VMEM((2, seq, RG_SLAB), F32), pltpu.VMEM((2, seq, RG_SLAB), F32),
                        pltpu.VMEM((seq, RG_SLAB), F32)],
        compiler_params=_cparams(("parallel", "arbitrary")),
        name="rglru",
    )(p, p, p, p, conv_w, conv_b, wg, gb, lam)


def _merge_kernel(ya_ref, yb_ref, ycl_ref, ycc_ref, ga_ref, gb_ref, gc_ref, h_ref, mod_ref, bw_ref,
                  wo_ref, o_ref, *, mod_base, n_lat_tiles):
    j = pl.program_id(1)

    @pl.when(j == 0)
    def _():
        o_ref[...] = jnp.zeros_like(o_ref)

    yc = jnp.where(pl.program_id(0) < n_lat_tiles, ycl_ref[...], ycc_ref[...])
    merged = (_sigmoid(ga_ref[...].astype(F32)) * _dot(ya_ref[...], bw_ref[0])
              + _sigmoid(gb_ref[...].astype(F32)) * _dot(yb_ref[...], bw_ref[1])
              + _sigmoid(gc_ref[...].astype(F32)) * _dot(yc, bw_ref[2]))
    o_ref[...] += _dot(merged.astype(BF16), wo_ref[...])

    @pl.when(j == pl.num_programs(1) - 1)
    def _():
        o_ref[...] = h_ref[...] + mod_ref[mod_base:mod_base + 1, :] * o_ref[...]


def _merge(h, n_rows, p, ya, yb, yc_lat, yc_ctx, modtab, mod_row, branch_w, w_out, layer, mod_base):
    d = h.shape[1]
    bw = ya.shape[1]
    n_j = d // MERGE_TILE
    gate0 = 9 * bw // MERGE_TILE
    per_branch = d // MERGE_TILE
    n_lat_tiles = yc_lat.shape[0] // MERGE_ROWS
    branch = pl.BlockSpec((MERGE_ROWS, bw), lambda i, j: (i, 0))
    branch_lat = pl.BlockSpec((MERGE_ROWS, bw), lambda i, j: (jnp.minimum(i, n_lat_tiles - 1), 0))
    branch_ctx = pl.BlockSpec((MERGE_ROWS, bw), lambda i, j: (jnp.maximum(i - n_lat_tiles, 0), 0))
    gate = lambda k: pl.BlockSpec((MERGE_ROWS, MERGE_TILE), lambda i, j: (i, gate0 + k * per_branch + j))
    return pl.pallas_call(
        functools.partial(_merge_kernel, mod_base=mod_base, n_lat_tiles=n_lat_tiles),
        grid=(n_rows // MERGE_ROWS, n_j),
        in_specs=[branch, branch, branch_lat, branch_ctx, gate(0), gate(1), gate(2),
                  pl.BlockSpec((MERGE_ROWS, d), lambda i, j: (i, 0)),
                  pl.BlockSpec((None, N_MOD, d), lambda i, j: (mod_row(MERGE_ROWS)(i), 0, 0)),
                  pl.BlockSpec((None, 3, bw, MERGE_TILE), lambda i, j: (layer, 0, 0, j)),
                  pl.BlockSpec((None, MERGE_TILE, d), lambda i, j: (layer, j, 0))],
        out_specs=pl.BlockSpec((MERGE_ROWS, d), lambda i, j: (i, 0)),
        out_shape=jax.ShapeDtypeStruct((n_rows, d), F32),
        compiler_params=_cparams(("parallel", "arbitrary")),
        name="merge",
    )(ya, yb, yc_lat, yc_ctx, p, p, p, h, modtab, branch_w, w_out)


def kernel(x, c, ctx, c_ctx, mod_w, mod_b, norm_g, ffn_w_in, ffn_w_out, w_in, mlstm_gate_b,
           hgrn_lb_logits, conv_w, conv_b, rg_gate_w, rg_gate_b, rg_lambda, branch_w, w_out, final_g):
    n_batch, seq, d = x.shape
    ctx_len = ctx.shape[1]
    depth = mod_w.shape[0]
    aw = branch_w.shape[2]
    dh = aw // A_HEADS
    n_lat_rows = n_batch * seq
    n_rows = n_lat_rows + n_batch * ctx_len
    assert seq % ROW_TILE == 0 and (n_batch * ctx_len) % ROW_TILE == 0
    assert seq % LA == 0 and ctx_len % LA == 0 and seq % LB == 0 and ctx_len % LB == 0
    assert seq % GRID_W == 0 and seq % RG_ROWS == 0 and n_lat_rows % ctx_len == 0
    assert n_batch + 1 <= MOD_ROWS and aw % RG_SLAB == 0 and RG_SLAB % (aw // C_BLOCKS) == 0

    def mod_row(tile):
        return lambda i: jnp.where(i < n_lat_rows // tile, 1 + i // (seq // tile), 0)

    c16 = jnp.zeros((MOD_ROWS, d), F32).at[0].set(c_ctx).at[1:1 + n_batch].set(c)
    modtab = _mod_table(c16, mod_w, mod_b).reshape(depth, MOD_ROWS, N_MOD, d)
    ffn_w_in_b = ffn_w_in.astype(BF16)
    ffn_w_out_b = ffn_w_out.astype(BF16)
    g0 = 4 * aw
    n_gate = 4 * A_HEADS
    b0 = g0 + n_gate
    w_in_b = w_in.astype(BF16)
    w_main = jnp.concatenate([w_in_b[:, :, :g0], w_in_b[:, :, b0:b0 + aw], w_in_b[:, :, b0 + 3 * aw:],
                              w_in_b[:, :, b0 + aw:b0 + 3 * aw]], axis=-1)
    n16 = (w_main.shape[2] - 2 * aw) // PROJ_TILE
    w_gate = jnp.pad(w_in_b[:, :, g0:g0 + n_gate], ((0, 0), (0, 0), (0, GATE_PAD - n_gate)))
    gate_bias = jnp.pad(mlstm_gate_b.reshape(depth, 1, n_gate), ((0, 0), (0, 0), (0, GATE_PAD - n_gate)))
    lb_p = jax.nn.softmax(hgrn_lb_logits.astype(F32), axis=0)
    lb_all = jnp.cumsum(lb_p, axis=0) - lb_p[0:1]
    branch_w_b = branch_w.astype(BF16)
    w_out_b = w_out.astype(BF16)
    n_slabs = aw // RG_SLAB
    c_db = aw // C_BLOCKS
    per_slab = RG_SLAB // c_db
    blocks = rg_gate_w.reshape(depth, 2, 2, n_slabs, per_slab, c_db, c_db)
    eye = jnp.eye(per_slab, dtype=F32)
    dense = jnp.einsum('ldgspio,pq->ldgspiqo', blocks, eye).reshape(depth, 2, 2, n_slabs, RG_SLAB, RG_SLAB)
    rg_w = dense.transpose(0, 3, 4, 1, 2, 5).reshape(depth, n_slabs, RG_SLAB, 4 * RG_SLAB).astype(BF16)
    rg_b = (rg_gate_b.reshape(depth, 2, 2, n_slabs, RG_SLAB).transpose(0, 3, 1, 2, 4)
            .reshape(depth, n_slabs, 1, 4 * RG_SLAB))
    tri = np.tril(np.ones((LA, LA), np.float32))
    tri3 = [jnp.asarray(np.concatenate([m] * A_PARTS, axis=1), BF16) for m in (tri, tri.T)]
    hgrn_consts = []
    for direction in (0, 1):
        m3, masks = _hgrn_constants(LB, direction)
        hgrn_consts.append((jnp.asarray(m3, BF16), jnp.asarray(masks, F32)))

    h = jnp.concatenate([x.reshape(n_lat_rows, d), ctx.reshape(n_batch * ctx_len, d)], axis=0)
    for layer in range(depth):
        last = layer == depth - 1
        mt = modtab[layer]
        ng = norm_g[layer]
        h = _ffn(h, n_rows, mt, mod_row, ng[0:1], ffn_w_in_b, ffn_w_out_b, layer, 0, 0)
        p, pf, gates = _inproj(h, mt, mod_row, ng[1:2], w_main, w_gate, layer, 3, n16)
        a_fwd = _mlstm(p, gates, gate_bias[layer], tri3[0], n_batch, seq, ctx_len, dh, 0)
        ya = _mlstm(p, gates, gate_bias[layer], tri3[1], n_batch, seq, ctx_len, dh, 1, o_fwd=a_fwd)
        b_fwd = _hgrn(p, pf, lb_all[layer, 0:1], hgrn_consts[0], n_batch, seq, ctx_len, 4, 0)
        yb = _hgrn(p, pf, lb_all[layer, 1:2], hgrn_consts[1], n_batch, seq, ctx_len, 4, 1, o_fwd=b_fwd)
        yc_lat, yc_ctx = _rglru(p, conv_w[layer], conv_b[layer][None, :], rg_w[layer], rg_b[layer],
                                rg_lambda[layer], n_batch, seq, ctx_len, 7)
        rows_out = n_lat_rows if last else n_rows
        h = _merge(h, rows_out, p, ya, yb, yc_lat, yc_ctx, mt, mod_row, branch_w_b, w_out_b, layer, 5)
        h = _ffn(h, rows_out, mt, mod_row, ng[2:3], ffn_w_in_b, ffn_w_out_b, layer, 1, 6,
                 final_g=final_g[None, :] if last else None)
    return h.reshape(n_batch, seq, d)
```
